```python
import math
import jax, jax.numpy as jnp
from jax import lax
import numpy as np

D_MODEL = 1024
BATCH = 8
SEQ = 4096
DEPTH = 1

HG_HEADS = 4
HG_DK = 128
HG_DV = 128
HG_WIDTH = HG_HEADS * HG_DK
HG_CHUNK = 64
DA_HEADS = 4
DA_HEAD = 64
DA_VDIM = 2 * DA_HEAD
DA_QK_WIDTH = DA_HEADS * 2 * DA_HEAD
DA_WIDTH = DA_HEADS * DA_VDIM
Q_BLOCK = 128
ROPE_THETA = 10000.0
N_GROUPS = 4
EXPERTS_PER_GROUP = 8
N_EXPERTS = N_GROUPS * EXPERTS_PER_GROUP
TOP_K = 2
D_FF_EXPERT = 512
MOE_BLOCK = 128
EPS = 1e-6
IN_COLS = 4 * HG_WIDTH + 2 * DA_QK_WIDTH + DA_WIDTH + 2 * D_MODEL

kernel_name = "hybrid_hgrn2_diffattn_hiermoe"


def rms_norm(x, gain):
    xf = x.astype(jnp.float32)
    y = xf * lax.rsqrt(jnp.mean(xf * xf, axis=-1, keepdims=True) + EPS)
    return (y * gain.astype(jnp.float32)).astype(x.dtype)


def rope(x, pos):
    half = x.shape[-1] // 2
    inv = ROPE_THETA ** (-jnp.arange(half, dtype=jnp.float32) / half)
    ang = pos.astype(jnp.float32)[:, None] * inv[None, :]
    cos = jnp.cos(ang)[None, :, None, :]
    sin = jnp.sin(ang)[None, :, None, :]
    x1, x2 = x[..., :half].astype(jnp.float32), x[..., half:].astype(jnp.float32)
    return jnp.concatenate([x1 * cos - x2 * sin, x1 * sin + x2 * cos], axis=-1).astype(x.dtype)


def hgrn2_chunked(q, k, v, log_f):
    B, S, H, DK = q.shape
    DV = v.shape[-1]
    C = HG_CHUNK
    NC = S // C

    def to_chunks(t):
        return t.astype(jnp.float32).reshape(B, NC, C, H, t.shape[-1]).transpose(1, 0, 3, 2, 4)

    qc, kc, vc, gc = to_chunks(q), to_chunks(k), to_chunks(v), to_chunks(log_f)
    causal = jnp.tril(jnp.ones((C, C), dtype=bool))

    def step(state, inp):
        qi, ki, vi, gi = inp
        b = jnp.cumsum(gi, axis=-2)
        b_ref = b[..., C // 2 - 1:C // 2, :]
        q_rel = qi * jnp.exp(b - b_ref)
        k_rel = ki * jnp.exp(b_ref - b)
        scores = jnp.einsum('bhtk,bhsk->bhts', q_rel, k_rel)
        scores = jnp.where(causal, scores, 0.0)
        o = (jnp.einsum('bhts,bhsv->bhtv', scores, vi)
             + jnp.einsum('bhtk,bhkv->bhtv', qi * jnp.exp(b), state))
        b_end = b[..., -1:, :]
        k_end = ki * jnp.exp(b_end - b)
        new_state = (jnp.exp(b_end[..., 0, :])[..., None] * state
                     + jnp.einsum('bhsk,bhsv->bhkv', k_end, vi))
        return new_state, o

    state0 = jnp.zeros((B, H, DK, DV), jnp.float32)
    _, o = lax.scan(step, state0, (qc, kc, vc, gc))
    return o.transpose(1, 0, 3, 2, 4).reshape(B, S, H, DV)


def diff_attention(q1, q2, k1, k2, v, lam):
    B, S, H, D = q1.shape
    NQ = S // Q_BLOCK
    scale = D ** -0.5
    key_pos = jnp.arange(S)

    def block(i):
        start = i * Q_BLOCK
        qb1 = lax.dynamic_slice_in_dim(q1, start, Q_BLOCK, axis=1)
        qb2 = lax.dynamic_slice_in_dim(q2, start, Q_BLOCK, axis=1)
        qpos = start + jnp.arange(Q_BLOCK)
        mask = key_pos[None, :] <= qpos[:, None]

        def probs(qb, kk):
            s = jnp.einsum('bqhd,bshd->bhqs', qb, kk).astype(jnp.float32) * scale
            return jax.nn.softmax(jnp.where(mask, s, -jnp.inf), axis=-1)

        a = probs(qb1, k1) - lam * probs(qb2, k2)
        return jnp.einsum('bhqs,bshv->bqhv', a.astype(v.dtype), v)

    o = lax.map(block, jnp.arange(NQ))
    return o.transpose(1, 0, 2, 3, 4).reshape(B, S, H, v.shape[-1])


def hier_moe(xn, w_rg, b_rg, w_re, b_re, w1, w3, w2):
    B, S, D = xn.shape
    T = B * S
    xt = xn.reshape(T, D)
    g_logits = (xt @ w_rg).astype(jnp.float32) + b_rg.astype(jnp.float32)
    g_sel = jnp.argmax(g_logits, axis=-1)
    g_w = jnp.max(jax.nn.softmax(g_logits, axis=-1), axis=-1, keepdims=True)
    e_logits = ((xt @ w_re).astype(jnp.float32) + b_re.astype(jnp.float32)).reshape(
        T, N_GROUPS, EXPERTS_PER_GROUP)
    e_in_group = jnp.sum(e_logits * jax.nn.one_hot(g_sel, N_GROUPS, dtype=jnp.float32)[:, :, None], axis=1)
    top_v, top_i = lax.top_k(e_in_group, TOP_K)
    e_w = jax.nn.softmax(top_v, axis=-1) * g_w

    A = T * TOP_K
    expert_id = (g_sel[:, None] * EXPERTS_PER_GROUP + top_i).reshape(A).astype(jnp.int32)
    token_id = jnp.repeat(jnp.arange(T, dtype=jnp.int32), TOP_K)
    weight = e_w.reshape(A)

    order = jnp.argsort(expert_id)
    se = expert_id[order]
    counts = jnp.zeros((N_EXPERTS,), jnp.int32).at[expert_id].add(1)
    padded = (counts + MOE_BLOCK - 1) // MOE_BLOCK * MOE_BLOCK
    pad_end = jnp.cumsum(padded)
    pad_start = pad_end - padded
    start = jnp.cumsum(counts) - counts
    dest = pad_start[se] + jnp.arange(A, dtype=jnp.int32) - start[se]
    P = A + N_EXPERTS * MOE_BLOCK
    NB = P // MOE_BLOCK
    buf_tok = jnp.full((P,), T, jnp.int32).at[dest].set(token_id[order])
    buf_w = jnp.zeros((P,), jnp.float32).at[dest].set(weight[order])
    block_start = jnp.arange(NB, dtype=jnp.int32) * MOE_BLOCK
    block_expert = jnp.minimum(
        jnp.sum(pad_end[None, :] <= block_start[:, None], axis=1), N_EXPERTS - 1).astype(jnp.int32)
    x_pad = jnp.concatenate([xt, jnp.zeros((1, D), xt.dtype)], axis=0)

    def run_block(args):
        tok, w, e = args
        xb = x_pad[tok]
        hid = jax.nn.silu(xb @ w1[e]) * (xb @ w3[e])
        return (hid @ w2[e]) * w[:, None].astype(xb.dtype)

    y = lax.map(run_block, (buf_tok.reshape(NB, MOE_BLOCK), buf_w.reshape(NB, MOE_BLOCK), block_expert))
    out = jnp.zeros((T + 1, D), xt.dtype).at[buf_tok].add(y.reshape(P, D))
    return out[:T].reshape(B, S, D)


def setup_inputs(seed: int = 0) -> dict:
    key = jax.random.key(seed)
    ks = jax.random.split(key, 20)

    def nrm(k, shape, scale):
        return jax.random.normal(k, shape, jnp.float32) * scale

    def gain(k, shape):
        return 1.0 + nrm(k, shape, 0.02)

    return {
        "x": nrm(ks[0], (BATCH, SEQ, D_MODEL), 1.0),
        "norm_mix": gain(ks[1], (DEPTH, D_MODEL)),
        "w_in": nrm(ks[2], (DEPTH, D_MODEL, IN_COLS), D_MODEL ** -0.5),
        "hg_lb": nrm(ks[3], (DEPTH + 1, HG_WIDTH), 0.1),
        "hg_out_norm": gain(ks[4], (DEPTH, HG_DV)),
        "da_q_norm": gain(ks[5], (DEPTH, DA_HEAD)),
        "da_k_norm": gain(ks[6], (DEPTH, DA_HEAD)),
        "da_lambda": nrm(ks[7], (DEPTH, 4, DA_HEAD), 0.1),
        "da_out_norm": gain(ks[8], (DEPTH, DA_VDIM)),
        "w_branch_hg": nrm(ks[9], (DEPTH, HG_WIDTH, D_MODEL), HG_WIDTH ** -0.5),
        "w_branch_da": nrm(ks[10], (DEPTH, DA_WIDTH, D_MODEL), DA_WIDTH ** -0.5),
        "w_out": nrm(ks[11], (DEPTH, D_MODEL, D_MODEL), D_MODEL ** -0.5),
        "norm_moe": gain(ks[12], (DEPTH, D_MODEL)),
        "w_router_group": nrm(ks[13], (DEPTH, D_MODEL, N_GROUPS), D_MODEL ** -0.5),
        "b_router_group": nrm(ks[14], (DEPTH, N_GROUPS), 0.01),
        "w_router_expert": nrm(ks[15], (DEPTH, D_MODEL, N_EXPERTS), D_MODEL ** -0.5),
        "b_router_expert": nrm(ks[16], (DEPTH, N_EXPERTS), 0.01),
        "w1": nrm(ks[17], (DEPTH, N_EXPERTS, D_MODEL, D_FF_EXPERT), D_MODEL ** -0.5),
        "w3": nrm(ks[18], (DEPTH, N_EXPERTS, D_MODEL, D_FF_EXPERT), D_MODEL ** -0.5),
        "w2": nrm(ks[19], (DEPTH, N_EXPERTS, D_FF_EXPERT, D_MODEL), D_FF_EXPERT ** -0.5),
    }


def reference(x, norm_mix, w_in, hg_lb, hg_out_norm, da_q_norm, da_k_norm, da_lambda,
              da_out_norm, w_branch_hg, w_branch_da, w_out, norm_moe, w_router_group,
              b_router_group, w_router_expert, b_router_expert, w1, w3, w2):
    B, S, _ = x.shape
    pos = jnp.arange(S)
    widths = [HG_WIDTH] * 4 + [DA_QK_WIDTH, DA_QK_WIDTH, DA_WIDTH, D_MODEL, D_MODEL]
    splits = [int(v) for v in np.cumsum(widths)[:-1]]
    lb_all = jnp.cumsum(jax.nn.softmax(hg_lb.astype(jnp.float32), axis=0), axis=0)

    for l in range(DEPTH):
        h = rms_norm(x, norm_mix[l])
        proj = h @ w_in[l]
        hq, hf, hi, hog, dq, dk, dv, gate_hg, gate_da = jnp.split(proj, splits, axis=-1)

        lb = lb_all[l]
        f = lb + (1.0 - lb) * jax.nn.sigmoid(hf.astype(jnp.float32))
        heads_hg = lambda t: t.reshape(B, S, HG_HEADS, t.shape[-1] // HG_HEADS)
        o_hg = hgrn2_chunked(heads_hg(hq), heads_hg(1.0 - f), heads_hg(hi), heads_hg(jnp.log(f)))
        o_hg = rms_norm(o_hg, hg_out_norm[l]) * jax.nn.silu(heads_hg(hog).astype(jnp.float32))
        y_hg = o_hg.reshape(B, S, HG_WIDTH).astype(x.dtype) @ w_branch_hg[l]

        qn = rope(rms_norm(dq.reshape(B, S, 2 * DA_HEADS, DA_HEAD), da_q_norm[l]), pos)
        kn = rope(rms_norm(dk.reshape(B, S, 2 * DA_HEADS, DA_HEAD), da_k_norm[l]), pos)
        qn = qn.reshape(B, S, DA_HEADS, 2, DA_HEAD)
        kn = kn.reshape(B, S, DA_HEADS, 2, DA_HEAD)
        v = dv.reshape(B, S, DA_HEADS, DA_VDIM)
        lam_p = da_lambda[l].astype(jnp.float32)
        lam_init = 0.8 - 0.6 * math.exp(-0.3 * l)
        lam = jnp.exp(jnp.sum(lam_p[0] * lam_p[1])) - jnp.exp(jnp.sum(lam_p[2] * lam_p[3])) + lam_init
        o_da = diff_attention(qn[..., 0, :], qn[..., 1, :], kn[..., 0, :], kn[..., 1, :], v, lam)
        o_da = rms_norm(o_da, da_out_norm[l]) * (1.0 - lam_init)
        y_da = o_da.reshape(B, S, DA_WIDTH).astype(x.dtype) @ w_branch_da[l]

        mixed = jax.nn.sigmoid(gate_hg) * y_hg + jax.nn.sigmoid(gate_da) * y_da
        x = x + (mixed @ w_out[l]).astype(x.dtype)

        x = x + hier_moe(rms_norm(x, norm_moe[l]), w_router_group[l], b_router_group[l],
                         w_router_expert[l], b_router_expert[l], w1[l], w3[l], w2[l]).astype(x.dtype)
    return x
```

```python
import functools
import math

import jax
import jax.numpy as jnp
from jax import lax
from jax.experimental import pallas as pl
from jax.experimental.pallas import tpu as pltpu

F32 = jnp.float32
BF16 = jnp.bfloat16

D_MODEL = 1024
HG_HEADS = 4
HG_DK = 128
HG_WIDTH = HG_HEADS * HG_DK
HG_CHUNK = 64
DA_HEADS = 4
DA_HEAD = 64
DA_VDIM = 2 * DA_HEAD
DA_WIDTH = DA_HEADS * DA_VDIM
ROPE_THETA = 10000.0
N_GROUPS = 4
EXPERTS_PER_GROUP = 8
N_EXPERTS = N_GROUPS * EXPERTS_PER_GROUP
TOP_K = 2
D_FF = 512
EPS = 1e-6
LAM_INIT = 0.8 - 0.6 * math.exp(-0.3 * 0)
IN_COLS = 4 * HG_WIDTH + 3 * DA_WIDTH + 2 * D_MODEL

LANES = 128
SUBLANES = 8
ROW_TILES = D_MODEL // LANES
NEG = -1e30

TM_PROJ = 256
TC_HGRN = 512
TQ_ATTN = 256
TM_MERGE = 256
TM_MOVE = 512
TM_EXPERT = 256
VMEM_LIMIT = 56 * 1024 * 1024


def _cparams(sem):
    return pltpu.CompilerParams(dimension_semantics=sem, vmem_limit_bytes=VMEM_LIMIT)


def _sigmoid(v):
    return 1.0 / (1.0 + jnp.exp(-v))


def _qk_prep(d, gain, cos, sin, gmat):
    ss = jnp.dot((d * d).astype(BF16), gmat, preferred_element_type=F32)
    y = d * lax.rsqrt(ss * (1.0 / DA_HEAD) + EPS) * gain
    lane = lax.broadcasted_iota(jnp.int32, (d.shape[0], LANES), 1)
    upper = (lane & (DA_HEAD // 2)) != 0
    outs = []
    for c in range(d.shape[1] // LANES):
        yc = y[:, c * LANES:(c + 1) * LANES]
        sw = jnp.where(upper, pltpu.roll(yc, DA_HEAD // 2, 1), pltpu.roll(yc, LANES - DA_HEAD // 2, 1))
        outs.append(yc * cos + sw * sin)
    return jnp.concatenate(outs, axis=1)


def _in_proj_kernel(x_ref, nm_ref, w_ref, lb_ref, gq_ref, gk_ref, cos_ref, sin_ref,
                    hq_ref, g_ref, kk_ref, hi_ref, og_ref, qn_ref, kn_ref, dv_ref, sgh_ref, sgd_ref):
    x = x_ref[...]
    h = x * lax.rsqrt(jnp.mean(x * x, axis=-1, keepdims=True) + EPS) * nm_ref[...]
    hb = h.astype(BF16)

    def proj(c0, width):
        return jnp.dot(hb, w_ref[:, c0:c0 + width], preferred_element_type=F32)

    hq_ref[...] = proj(0, HG_WIDTH).astype(BF16)

    hf = proj(HG_WIDTH, HG_WIDTH)
    lbp = lb_ref[...]
    mx = jnp.maximum(lbp[0:1], lbp[1:2])
    e0 = jnp.exp(lbp[0:1] - mx)
    e1 = jnp.exp(lbp[1:2] - mx)
    lb = e0 / (e0 + e1)
    f = lb + (1.0 - lb) * _sigmoid(hf)
    g_ref[...] = jnp.log(f)
    kk_ref[...] = (1.0 - f).astype(BF16)

    hi_ref[...] = proj(2 * HG_WIDTH, HG_WIDTH).astype(BF16)
    hog = proj(3 * HG_WIDTH, HG_WIDTH)
    og_ref[...] = (hog * _sigmoid(hog)).astype(BF16)

    r = lax.broadcasted_iota(jnp.int32, (DA_WIDTH, DA_WIDTH), 0) // DA_HEAD
    c = lax.broadcasted_iota(jnp.int32, (DA_WIDTH, DA_WIDTH), 1) // DA_HEAD
    gmat = jnp.where(r == c, 1.0, 0.0).astype(BF16)
    cos = cos_ref[...]
    sin = sin_ref[...]
    base = 4 * HG_WIDTH
    qn_ref[...] = _qk_prep(proj(base, DA_WIDTH), gq_ref[...], cos, sin, gmat).astype(BF16)
    kn_ref[...] = _qk_prep(proj(base + DA_WIDTH, DA_WIDTH), gk_ref[...], cos, sin, gmat).astype(BF16)
    dv_ref[...] = proj(base + 2 * DA_WIDTH, DA_WIDTH).astype(BF16)
    base += 3 * DA_WIDTH
    sgh_ref[...] = _sigmoid(proj(base, D_MODEL)).astype(BF16)
    sgd_ref[...] = _sigmoid(proj(base + D_MODEL, D_MODEL)).astype(BF16)


def _in_proj(x2, norm_mix, w_in_bf, hg_lb, gq, gk, cos_t, sin_t, seq):
    t = x2.shape[0]
    tm = TM_PROJ
    nseq = seq // tm
    row = lambda w: pl.BlockSpec((tm, w), lambda i: (i, 0))
    const = lambda shape: pl.BlockSpec(shape, lambda i: (0, 0))
    tab = pl.BlockSpec((tm, LANES), lambda i: (i % nseq, 0))
    out_shape = [
        jax.ShapeDtypeStruct((t, HG_WIDTH), BF16),
        jax.ShapeDtypeStruct((t, HG_WIDTH), F32),
        jax.ShapeDtypeStruct((t, HG_WIDTH), BF16),
        jax.ShapeDtypeStruct((t, HG_WIDTH), BF16),
        jax.ShapeDtypeStruct((t, HG_WIDTH), BF16),
        jax.ShapeDtypeStruct((t, DA_WIDTH), BF16),
        jax.ShapeDtypeStruct((t, DA_WIDTH), BF16),
        jax.ShapeDtypeStruct((t, DA_WIDTH), BF16),
        jax.ShapeDtypeStruct((t, D_MODEL), BF16),
        jax.ShapeDtypeStruct((t, D_MODEL), BF16),
    ]
    out_specs = [row(HG_WIDTH)] * 5 + [row(DA_WIDTH)] * 3 + [row(D_MODEL)] * 2
    return pl.pallas_call(
        _in_proj_kernel,
        grid=(t // tm,),
        in_specs=[row(D_MODEL), const((1, D_MODEL)), const((D_MODEL, IN_COLS)), const((2, HG_WIDTH)),
                  const((1, DA_WIDTH)), const((1, DA_WIDTH)), tab, tab],
        out_specs=out_specs,
        out_shape=out_shape,
        compiler_params=_cparams(("parallel",)),
        name="in_proj",
    )(x2, norm_mix, w_in_bf, hg_lb, gq, gk, cos_t, sin_t)


def _hgrn_kernel(q_ref, g_ref, k_ref, v_ref, og_ref, gain_ref, o_ref, st_ref):
    @pl.when(pl.program_id(1) == 0)
    def _():
        st_ref[...] = jnp.zeros_like(st_ref)

    c = HG_CHUNK
    row = lax.broadcasted_iota(jnp.int32, (c, c), 0)
    col = lax.broadcasted_iota(jnp.int32, (c, c), 1)
    causal = col <= row
    tri = jnp.where(causal, 1.0, 0.0).astype(BF16)
    gain = gain_ref[...]
    nt = (((1,), (1,)), ((), ()))
    tn = (((0,), (0,)), ((), ()))

    def chunk(ci, carry):
        rows = pl.ds(pl.multiple_of(ci * c, c), c)
        for h in range(HG_HEADS):
            cols = slice(h * HG_DK, (h + 1) * HG_DK)
            gc = g_ref[rows, cols]
            g_hi = gc.astype(BF16)
            g_lo = (gc - g_hi.astype(F32)).astype(BF16)
            b = (jnp.dot(tri, g_hi, preferred_element_type=F32)
                 + jnp.dot(tri, g_lo, preferred_element_type=F32))
            b_mid = b[c // 2 - 1:c // 2, :]
            b_end = b[c - 1:c, :]
            qc = q_ref[rows, cols].astype(F32)
            kc = k_ref[rows, cols].astype(F32)
            vc = v_ref[rows, cols]
            q_rel = (qc * jnp.exp(b - b_mid)).astype(BF16)
            k_rel = (kc * jnp.exp(b_mid - b)).astype(BF16)
            sc = lax.dot_general(q_rel, k_rel, nt, preferred_element_type=F32)
            sc = jnp.where(causal, sc, 0.0).astype(BF16)
            q_st = (qc * jnp.exp(b)).astype(BF16)
            st = st_ref[h]
            o = (jnp.dot(sc, vc, preferred_element_type=F32)
                 + lax.dot_general(q_st, st.astype(BF16), nt, preferred_element_type=F32))
            k_end = (kc * jnp.exp(b_end - b)).astype(BF16)
            st_ref[h] = st * jnp.exp(b_end) + lax.dot_general(vc, k_end, tn, preferred_element_type=F32)
            on = o * lax.rsqrt(jnp.mean(o * o, axis=-1, keepdims=True) + EPS) * gain
            o_ref[rows, cols] = (on * og_ref[rows, cols].astype(F32)).astype(BF16)
        return carry

    lax.fori_loop(0, q_ref.shape[0] // c, chunk, 0)


def _hgrn(hq, g, kk, hi, og, gain, batch, seq):
    t = hq.shape[0]
    tc = TC_HGRN
    ns = seq // tc
    spec = pl.BlockSpec((tc, HG_WIDTH), lambda b, s: (b * ns + s, 0))
    return pl.pallas_call(
        _hgrn_kernel,
        grid=(batch, ns),
        in_specs=[spec] * 5 + [pl.BlockSpec((1, HG_DK), lambda b, s: (0, 0))],
        out_specs=spec,
        out_shape=jax.ShapeDtypeStruct((t, HG_WIDTH), BF16),
        scratch_shapes=[pltpu.VMEM((HG_HEADS, HG_DK, HG_DK), F32)],
        compiler_params=_cparams(("parallel", "arbitrary")),
        name="hgrn",
    )(hq, g, kk, hi, og, gain)


def _attn_kernel(lam_ref, q_ref, k_ref, v_ref, gain_ref, o_ref, m_ref, l_ref, acc_ref):
    tq = q_ref.shape[0]
    i = pl.program_id(2)
    q = q_ref[...]
    lane = lax.broadcasted_iota(jnp.int32, q.shape, 1)
    zero = jnp.zeros_like(q)
    qq = jnp.concatenate([jnp.where(lane < DA_HEAD, q, zero), jnp.where(lane >= DA_HEAD, q, zero)], axis=0)
    m_ref[...] = jnp.full_like(m_ref, NEG)
    l_ref[...] = jnp.zeros_like(l_ref)
    acc_ref[...] = jnp.zeros_like(acc_ref)
    nt = (((1,), (1,)), ((), ()))

    def step(j, masked):
        rows = pl.ds(pl.multiple_of(j * tq, tq), tq)
        k = k_ref[rows, :]
        v = v_ref[rows, :]
        s = lax.dot_general(qq, k, nt, preferred_element_type=F32)
        if masked:
            r = lax.broadcasted_iota(jnp.int32, s.shape, 0)
            c = lax.broadcasted_iota(jnp.int32, s.shape, 1)
            s = jnp.where(c <= jnp.where(r >= tq, r - tq, r), s, NEG)
        m_prev = m_ref[...]
        m_new = jnp.maximum(m_prev, jnp.max(s, axis=1, keepdims=True))
        alpha = jnp.exp(m_prev - m_new)
        p = jnp.exp(s - m_new)
        l_ref[...] = alpha * l_ref[...] + jnp.sum(p, axis=1, keepdims=True)
        acc_ref[...] = alpha * acc_ref[...] + jnp.dot(p.astype(BF16), v, preferred_element_type=F32)
        m_ref[...] = m_new

    def body(j, carry):
        step(j, False)
        return carry

    lax.fori_loop(0, i, body, 0)
    step(i, True)

    lp = lam_ref[...]
    lam = (jnp.exp(jnp.sum(lp[0:1] * lp[1:2], axis=1, keepdims=True))
           - jnp.exp(jnp.sum(lp[2:3] * lp[3:4], axis=1, keepdims=True)) + LAM_INIT)
    acc = acc_ref[...]
    l = l_ref[...]
    o = acc[:tq] / l[:tq] - lam * (acc[tq:] / l[tq:])
    on = o * lax.rsqrt(jnp.mean(o * o, axis=-1, keepdims=True) + EPS) * gain_ref[...] * (1.0 - LAM_INIT)
    o_ref[...] = on.astype(BF16)


def _attn(lam_p, qn, kn, dv, gain, batch, seq):
    t = qn.shape[0]
    tq = TQ_ATTN
    nq = seq // tq
    return pl.pallas_call(
        _attn_kernel,
        grid=(batch, DA_HEADS, nq),
        in_specs=[pl.BlockSpec((4, DA_HEAD), lambda b, h, i: (0, 0)),
                  pl.BlockSpec((tq, DA_VDIM), lambda b, h, i: (b * nq + i, h)),
                  pl.BlockSpec((seq, DA_VDIM), lambda b, h, i: (b, h)),
                  pl.BlockSpec((seq, DA_VDIM), lambda b, h, i: (b, h)),
                  pl.BlockSpec((1, DA_VDIM), lambda b, h, i: (0, 0))],
        out_specs=pl.BlockSpec((tq, DA_VDIM), lambda b, h, i: (b * nq + i, h)),
        out_shape=jax.ShapeDtypeStruct((t, DA_WIDTH), BF16),
        scratch_shapes=[pltpu.VMEM((2 * tq, 1), F32), pltpu.VMEM((2 * tq, 1), F32),
                        pltpu.VMEM((2 * tq, DA_VDIM), F32)],
        compiler_params=_cparams(("parallel", "parallel", "arbitrary")),
        name="attn",
    )(lam_p, qn, kn, dv, gain)


def _merge_kernel(ohg_ref, oda_ref, sgh_ref, sgd_ref, x_ref, wbh_ref, wbd_ref, wo_ref, nm_ref, wr_ref, br_ref,
                  xmid_ref, xn_ref, meta_ref, cnt_ref, carry_ref):
    tm = x_ref.shape[0]

    @pl.when(pl.program_id(0) == 0)
    def _():
        carry_ref[...] = jnp.zeros_like(carry_ref)

    y_hg = jnp.dot(ohg_ref[...], wbh_ref[...], preferred_element_type=F32)
    y_da = jnp.dot(oda_ref[...], wbd_ref[...], preferred_element_type=F32)
    mixed = sgh_ref[...].astype(F32) * y_hg + sgd_ref[...].astype(F32) * y_da
    xm = x_ref[...] + jnp.dot(mixed.astype(BF16), wo_ref[...], preferred_element_type=F32)
    xmid_ref[...] = xm
    xn = xm * lax.rsqrt(jnp.mean(xm * xm, axis=-1, keepdims=True) + EPS) * nm_ref[...]
    for c in range(ROW_TILES):
        xn_ref[pl.ds(c, tm, stride=ROW_TILES), :] = xn[:, c * LANES:(c + 1) * LANES]

    w = wr_ref[...]
    w_hi = w.astype(BF16)
    w_lo = (w - w_hi.astype(F32)).astype(BF16)
    x_hi = xn.astype(BF16)
    x_lo = (xn - x_hi.astype(F32)).astype(BF16)
    logit = (jnp.dot(x_hi, w_hi, preferred_element_type=F32) + jnp.dot(x_hi, w_lo, preferred_element_type=F32)
             + jnp.dot(x_lo, w_hi, preferred_element_type=F32) + br_ref[...])

    lane = lax.broadcasted_iota(jnp.int32, (tm, LANES), 1)
    lanef = lane.astype(F32)
    big = float(LANES)
    is_g = lane < N_GROUPS
    gl = jnp.where(is_g, logit, NEG)
    gmax = jnp.max(gl, axis=1, keepdims=True)
    g_sel = jnp.min(jnp.where(gl == gmax, lanef, big), axis=1, keepdims=True)
    g_w = 1.0 / jnp.sum(jnp.where(is_g, jnp.exp(gl - gmax), 0.0), axis=1, keepdims=True)
    lo = N_GROUPS + EXPERTS_PER_GROUP * g_sel
    in_grp = (lanef >= lo) & (lanef < lo + EXPERTS_PER_GROUP)
    el = jnp.where(in_grp, logit, NEG)
    v1 = jnp.max(el, axis=1, keepdims=True)
    i1 = jnp.min(jnp.where(el == v1, lanef, big), axis=1, keepdims=True)
    el2 = jnp.where(lanef == i1, NEG, el)
    v2 = jnp.max(el2, axis=1, keepdims=True)
    i2 = jnp.min(jnp.where(el2 == v2, lanef, big), axis=1, keepdims=True)
    e1 = i1 - N_GROUPS
    e2 = i2 - N_GROUPS
    d = jnp.exp(v2 - v1)
    w1 = g_w / (1.0 + d)
    w2 = g_w * d / (1.0 + d)

    oh1 = jnp.where(lanef == e1, 1.0, 0.0)
    oh2 = jnp.where(lanef == e2, 1.0, 0.0)
    r = lax.broadcasted_iota(jnp.int32, (tm, tm), 0)
    c = lax.broadcasted_iota(jnp.int32, (tm, tm), 1)
    below = jnp.where(c < r, 1.0, 0.0).astype(BF16)
    c1 = jnp.dot(below, oh1.astype(BF16), preferred_element_type=F32)
    c2 = jnp.dot(below, oh2.astype(BF16), preferred_element_type=F32)
    tot1 = jnp.sum(oh1, axis=0, keepdims=True)
    tot2 = jnp.sum(oh2, axis=0, keepdims=True)
    carry = carry_ref[...]
    rank1 = jnp.sum(oh1 * (c1 + carry), axis=1, keepdims=True)
    rank2 = jnp.sum(oh2 * (c2 + carry + tot1), axis=1, keepdims=True)
    carry = carry + tot1 + tot2
    carry_ref[...] = carry
    cnt_ref[...] = jnp.broadcast_to(carry, cnt_ref.shape)

    meta = jnp.where(lane == 0, e1, jnp.where(lane == 1, e2, jnp.where(lane == 2, rank1, jnp.where(
        lane == 3, rank2, jnp.where(lane == 4, w1, jnp.where(lane == 5, w2, 0.0))))))
    meta_ref[...] = meta


def _merge(o_hg, o_da, sgh, sgd, x2, wbh, wbd, wo, norm_moe, wr, br):
    t = x2.shape[0]
    tm = TM_MERGE
    row = lambda w: pl.BlockSpec((tm, w), lambda i: (i, 0))
    const = lambda shape: pl.BlockSpec(shape, lambda i: (0, 0))
    return pl.pallas_call(
        _merge_kernel,
        grid=(t // tm,),
        in_specs=[row(HG_WIDTH), row(DA_WIDTH), row(D_MODEL), row(D_MODEL), row(D_MODEL),
                  const((HG_WIDTH, D_MODEL)), const((DA_WIDTH, D_MODEL)), const((D_MODEL, D_MODEL)),
                  const((1, D_MODEL)), const((D_MODEL, LANES)), const((1, LANES))],
        out_specs=[row(D_MODEL), pl.BlockSpec((tm * ROW_TILES, LANES), lambda i: (i, 0)), row(LANES),
                   const((SUBLANES, LANES))],
        out_shape=[jax.ShapeDtypeStruct((t, D_MODEL), F32),
                   jax.ShapeDtypeStruct((t * ROW_TILES, LANES), F32),
                   jax.ShapeDtypeStruct((t, LANES), F32),
                   jax.ShapeDtypeStruct((SUBLANES, LANES), F32)],
        scratch_shapes=[pltpu.VMEM((1, LANES), F32)],
        compiler_params=_cparams(("arbitrary",)),
        name="merge",
    )(o_hg, o_da, sgh, sgd, x2, wbh, wbd, wo, norm_moe, wr, br)


def _dispatch_kernel(pos_hbm, xn_ref, xg_in, xg_out, pos_smem, sem_p, sem_o):
    del xg_in
    tm = xn_ref.shape[0]
    i = pl.program_id(0)
    cp = pltpu.make_async_copy(pos_hbm.at[i], pos_smem, sem_p)
    cp.start()
    cp.wait()

    def row_copy(r, k):
        return pltpu.make_async_copy(xn_ref.at[r], xg_out.at[pos_smem[k * tm + r]], sem_o)

    def start(r, carry):
        row_copy(r, 0).start()
        row_copy(r, 1).start()
        return carry

    def wait(r, carry):
        row_copy(r, 0).wait()
        row_copy(r, 1).wait()
        return carry

    lax.fori_loop(0, tm, start, 0)
    lax.fori_loop(0, tm, wait, 0)


def _dispatch(pos_tiles, xn3, n_rows):
    t = xn3.shape[0]
    tm = TM_MOVE
    xg0 = jnp.zeros((n_rows, ROW_TILES, LANES), F32)
    return pl.pallas_call(
        _dispatch_kernel,
        grid=(t // tm,),
        in_specs=[pl.BlockSpec(memory_space=pl.ANY),
                  pl.BlockSpec((tm, ROW_TILES, LANES), lambda i: (i, 0, 0)),
                  pl.BlockSpec(memory_space=pl.ANY)],
        out_specs=pl.BlockSpec(memory_space=pl.ANY),
        out_shape=jax.ShapeDtypeStruct((n_rows, ROW_TILES, LANES), F32),
        scratch_shapes=[pltpu.SMEM((TOP_K * tm,), jnp.int32), pltpu.SemaphoreType.DMA, pltpu.SemaphoreType.DMA],
        input_output_aliases={2: 0},
        compiler_params=_cparams(("arbitrary",)),
        name="dispatch",
    )(pos_tiles, xn3, xg0)


def _expert_kernel(be_ref, xg_ref, w1_ref, w3_ref, w2_ref, y_ref):
    del be_ref
    tm = xg_ref.shape[0] // ROW_TILES
    xb = jnp.concatenate([xg_ref[pl.ds(c, tm, stride=ROW_TILES), :] for c in range(ROW_TILES)], axis=1).astype(BF16)
    a = jnp.dot(xb, w1_ref[0], preferred_element_type=F32)
    b = jnp.dot(xb, w3_ref[0], preferred_element_type=F32)
    hid = (a * _sigmoid(a) * b).astype(BF16)
    y = jnp.dot(hid, w2_ref[0], preferred_element_type=F32)
    for c in range(ROW_TILES):
        y_ref[pl.ds(c, tm, stride=ROW_TILES), :] = y[:, c * LANES:(c + 1) * LANES]


def _experts(block_expert, xg2, w1, w3, w2):
    tm = TM_EXPERT
    nb = xg2.shape[0] // (tm * ROW_TILES)
    grid_spec = pltpu.PrefetchScalarGridSpec(
        num_scalar_prefetch=1,
        grid=(nb,),
        in_specs=[pl.BlockSpec((tm * ROW_TILES, LANES), lambda i, be: (i, 0)),
                  pl.BlockSpec((1, D_MODEL, D_FF), lambda i, be: (be[i], 0, 0)),
                  pl.BlockSpec((1, D_MODEL, D_FF), lambda i, be: (be[i], 0, 0)),
                  pl.BlockSpec((1, D_FF, D_MODEL), lambda i, be: (be[i], 0, 0))],
        out_specs=pl.BlockSpec((tm * ROW_TILES, LANES), lambda i, be: (i, 0)),
    )
    return pl.pallas_call(
        _expert_kernel,
        grid_spec=grid_spec,
        out_shape=jax.ShapeDtypeStruct(xg2.shape, F32),
        compiler_params=_cparams(("arbitrary",)),
        name="experts",
    )(block_expert, xg2, w1, w3, w2)


def _combine_kernel(pos_hbm, meta_ref, xmid_ref, y_hbm, o_ref, pos_smem, buf_ref, sem_p, sem_i):
    tm = xmid_ref.shape[0]
    i = pl.program_id(0)
    cp = pltpu.make_async_copy(pos_hbm.at[i], pos_smem, sem_p)
    cp.start()
    cp.wait()

    def row_copy(r, k):
        dst = buf_ref.at[k, pl.ds(pl.multiple_of(r * ROW_TILES, ROW_TILES), ROW_TILES), :]
        return pltpu.make_async_copy(y_hbm.at[pos_smem[k * tm + r]], dst, sem_i)

    def start(r, carry):
        row_copy(r, 0).start()
        row_copy(r, 1).start()
        return carry

    def wait(r, carry):
        row_copy(r, 0).wait()
        row_copy(r, 1).wait()
        return carry

    lax.fori_loop(0, tm, start, 0)
    lax.fori_loop(0, tm, wait, 0)

    meta = meta_ref[...]
    w1 = meta[:, 4:5]
    w2 = meta[:, 5:6]
    for c in range(ROW_TILES):
        cols = slice(c * LANES, (c + 1) * LANES)
        y1 = buf_ref[0, pl.ds(c, tm, stride=ROW_TILES), :]
        y2 = buf_ref[1, pl.ds(c, tm, stride=ROW_TILES), :]
        o_ref[:, cols] = xmid_ref[:, cols] + (w1 * y1 + w2 * y2)


def _combine(pos_tiles, meta, xmid, y3):
    t = xmid.shape[0]
    tm = TM_MOVE
    return pl.pallas_call(
        _combine_kernel,
        grid=(t // tm,),
        in_specs=[pl.BlockSpec(memory_space=pl.ANY),
                  pl.BlockSpec((tm, LANES), lambda i: (i, 0)),
                  pl.BlockSpec((tm, D_MODEL), lambda i: (i, 0)),
                  pl.BlockSpec(memory_space=pl.ANY)],
        out_specs=pl.BlockSpec((tm, D_MODEL), lambda i: (i, 0)),
        out_shape=jax.ShapeDtypeStruct((t, D_MODEL), F32),
        scratch_shapes=[pltpu.SMEM((TOP_K * tm,), jnp.int32),
                        pltpu.VMEM((TOP_K, tm * ROW_TILES, LANES), F32),
                        pltpu.SemaphoreType.DMA, pltpu.SemaphoreType.DMA],
        compiler_params=_cparams(("arbitrary",)),
        name="combine",
    )(pos_tiles, meta, xmid, y3)


def kernel(x, norm_mix, w_in, hg_lb, hg_out_norm, da_q_norm, da_k_norm, da_lambda, da_out_norm, w_branch_hg,
           w_branch_da, w_out, norm_moe, w_router_group, b_router_group, w_router_expert, b_router_expert,
           w1, w3, w2):
    batch, seq, d = x.shape
    assert d == D_MODEL and norm_mix.shape[0] == 1 and w_in.shape[2] == IN_COLS
    assert seq % TC_HGRN == 0 and seq % TQ_ATTN == 0 and (batch * seq) % TM_MOVE == 0
    t = batch * seq
    x2 = x.reshape(t, d)

    half = DA_HEAD // 2
    inv = ROPE_THETA ** (-jnp.arange(half, dtype=F32) / half)
    ang = jnp.arange(seq, dtype=F32)[:, None] * inv[None, :]
    cos_t = jnp.tile(jnp.cos(ang), (1, 2 * LANES // DA_HEAD))
    sin_t = jnp.tile(jnp.concatenate([-jnp.sin(ang), jnp.sin(ang)], axis=1), (1, LANES // DA_HEAD))

    reps = DA_WIDTH // DA_HEAD
    gq = jnp.tile(da_q_norm[0].astype(F32) * (DA_HEAD ** -0.5), reps)[None, :]
    gk = jnp.tile(da_k_norm[0].astype(F32), reps)[None, :]

    hq, g, kk, hi, og, qn, kn, dv, sgh, sgd = _in_proj(
        x2, norm_mix.astype(F32), w_in[0].astype(BF16), hg_lb.astype(F32), gq, gk, cos_t, sin_t, seq)

    o_hg = _hgrn(hq, g, kk, hi, og, hg_out_norm.astype(F32), batch, seq)
    o_da = _attn(da_lambda[0].astype(F32), qn, kn, dv, da_out_norm.astype(F32), batch, seq)

    wr = jnp.zeros((D_MODEL, LANES), F32)
    wr = wr.at[:, :N_GROUPS].set(w_router_group[0]).at[:, N_GROUPS:N_GROUPS + N_EXPERTS].set(w_router_expert[0])
    br = jnp.zeros((1, LANES), F32)
    br = br.at[0, :N_GROUPS].set(b_router_group[0]).at[0, N_GROUPS:N_GROUPS + N_EXPERTS].set(b_router_expert[0])

    xmid, xn_rm, meta, cnt = _merge(o_hg, o_da, sgh, sgd, x2, w_branch_hg[0].astype(BF16),
                                    w_branch_da[0].astype(BF16), w_out[0].astype(BF16), norm_moe.astype(F32), wr, br)

    tmx = TM_EXPERT
    counts = cnt[0, :N_EXPERTS].astype(jnp.int32)
    padded = (counts + tmx - 1) // tmx * tmx
    pad_end = jnp.cumsum(padded)
    pad_start = pad_end - padded
    eid = meta[:, 0:TOP_K].astype(jnp.int32)
    rank = meta[:, TOP_K:2 * TOP_K].astype(jnp.int32)
    pos = pad_start[eid] + rank
    n_rows = t * TOP_K + N_EXPERTS * tmx
    nb = n_rows // tmx
    block_start = jnp.arange(nb, dtype=jnp.int32) * tmx
    block_expert = jnp.minimum(jnp.sum(pad_end[None, :] <= block_start[:, None], axis=1),
                               N_EXPERTS - 1).astype(jnp.int32)
    nt = t // TM_MOVE
    pos_tiles = pos.T.reshape(TOP_K, nt, TM_MOVE).transpose(1, 0, 2).reshape(nt, TOP_K * TM_MOVE)

    xg = _dispatch(pos_tiles, xn_rm.reshape(t, ROW_TILES, LANES), n_rows)
    y = _experts(block_expert, xg.reshape(n_rows * ROW_TILES, LANES), w1[0].astype(BF16), w3[0].astype(BF16),
                 w2[0].astype(BF16))
    out = _combine(pos_tiles, meta, xmid, y.reshape(n_rows, ROW_TILES, LANES))
    return out.reshape(batch, seq, d)
```

```python
import functools
import math

import jax
import jax.numpy as jnp
from jax import lax
from jax.experimental import pallas as pl
from jax.experimental.pallas import tpu as pltpu

F32 = jnp.float32
BF16 = jnp.bfloat16

D_MODEL = 1024
HG_HEADS = 4
HG_DK = 128
HG_WIDTH = HG_HEADS * HG_DK
HG_CHUNK = 64
DA_HEADS = 4
DA_HEAD = 64
DA_VDIM = 2 * DA_HEAD
DA_WIDTH = DA_HEADS * DA_VDIM
ROPE_THETA = 10000.0
N_GROUPS = 4
EXPERTS_PER_GROUP = 8
N_EXPERTS = N_GROUPS * EXPERTS_PER_GROUP
TOP_K = 2
D_FF = 512
EPS = 1e-6
LAM_INIT = 0.8 - 0.6 * math.exp(-0.3 * 0)
IN_COLS = 4 * HG_WIDTH + 3 * DA_WIDTH + 2 * D_MODEL

LANES = 128
SUBLANES = 8
ROW_TILES = D_MODEL // LANES
NEG = -1e30

TM_PROJ = 256
TC_HGRN = 512
HGRN_UNROLL = 4
TQ_ATTN = 512
TM_MERGE = 256
TM_MOVE = 512
TM_EXPERT = 256
VMEM_LIMIT = 56 * 1024 * 1024


def _cparams(sem):
    return pltpu.CompilerParams(dimension_semantics=sem, vmem_limit_bytes=VMEM_LIMIT)


def _sigmoid(v):
    return 1.0 / (1.0 + jnp.exp(-v))


def _qk_prep(d, gain, cos, sin, gmat):
    ss = jnp.dot((d * d).astype(BF16), gmat, preferred_element_type=F32)
    y = d * lax.rsqrt(ss * (1.0 / DA_HEAD) + EPS) * gain
    lane = lax.broadcasted_iota(jnp.int32, (d.shape[0], LANES), 1)
    upper = (lane & (DA_HEAD // 2)) != 0
    outs = []
    for c in range(d.shape[1] // LANES):
        yc = y[:, c * LANES:(c + 1) * LANES]
        sw = jnp.where(upper, pltpu.roll(yc, DA_HEAD // 2, 1), pltpu.roll(yc, LANES - DA_HEAD // 2, 1))
        outs.append(yc * cos + sw * sin)
    return jnp.concatenate(outs, axis=1)


def _in_proj_kernel(x_ref, nm_ref, w_ref, lb_ref, gq_ref, gk_ref, cos_ref, sin_ref,
                    hq_ref, g_ref, kk_ref, hi_ref, og_ref, qn_ref, kn_ref, dv_ref, sgh_ref, sgd_ref):
    x = x_ref[...]
    h = x * lax.rsqrt(jnp.mean(x * x, axis=-1, keepdims=True) + EPS) * nm_ref[...]
    hb = h.astype(BF16)

    def proj(c0, width):
        return jnp.dot(hb, w_ref[:, c0:c0 + width], preferred_element_type=F32)

    hq_ref[...] = proj(0, HG_WIDTH).astype(BF16)

    hf = proj(HG_WIDTH, HG_WIDTH)
    lbp = lb_ref[...]
    mx = jnp.maximum(lbp[0:1], lbp[1:2])
    e0 = jnp.exp(lbp[0:1] - mx)
    e1 = jnp.exp(lbp[1:2] - mx)
    lb = e0 / (e0 + e1)
    f = lb + (1.0 - lb) * _sigmoid(hf)
    g_ref[...] = jnp.log(f)
    kk_ref[...] = (1.0 - f).astype(BF16)

    hi_ref[...] = proj(2 * HG_WIDTH, HG_WIDTH).astype(BF16)
    hog = proj(3 * HG_WIDTH, HG_WIDTH)
    og_ref[...] = (hog * _sigmoid(hog)).astype(BF16)

    r = lax.broadcasted_iota(jnp.int32, (DA_WIDTH, DA_WIDTH), 0) // DA_HEAD
    c = lax.broadcasted_iota(jnp.int32, (DA_WIDTH, DA_WIDTH), 1) // DA_HEAD
    gmat = jnp.where(r == c, 1.0, 0.0).astype(BF16)
    cos = cos_ref[...]
    sin = sin_ref[...]
    base = 4 * HG_WIDTH
    qn_ref[...] = _qk_prep(proj(base, DA_WIDTH), gq_ref[...], cos, sin, gmat).astype(BF16)
    kn_ref[...] = _qk_prep(proj(base + DA_WIDTH, DA_WIDTH), gk_ref[...], cos, sin, gmat).astype(BF16)
    dv_ref[...] = proj(base + 2 * DA_WIDTH, DA_WIDTH).astype(BF16)
    base += 3 * DA_WIDTH
    sgh_ref[...] = _sigmoid(proj(base, D_MODEL)).astype(BF16)
    sgd_ref[...] = _sigmoid(proj(base + D_MODEL, D_MODEL)).astype(BF16)


def _in_proj(x2, norm_mix, w_in_bf, hg_lb, gq, gk, cos_t, sin_t, seq):
    t = x2.shape[0]
    tm = TM_PROJ
    nseq = seq // tm
    row = lambda w: pl.BlockSpec((tm, w), lambda i: (i, 0))
    const = lambda shape: pl.BlockSpec(shape, lambda i: (0, 0))
    tab = pl.BlockSpec((tm, LANES), lambda i: (i % nseq, 0))
    out_shape = [
        jax.ShapeDtypeStruct((t, HG_WIDTH), BF16),
        jax.ShapeDtypeStruct((t, HG_WIDTH), F32),
        jax.ShapeDtypeStruct((t, HG_WIDTH), BF16),
        jax.ShapeDtypeStruct((t, HG_WIDTH), BF16),
        jax.ShapeDtypeStruct((t, HG_WIDTH), BF16),
        jax.ShapeDtypeStruct((t, DA_WIDTH), BF16),
        jax.ShapeDtypeStruct((t, DA_WIDTH), BF16),
        jax.ShapeDtypeStruct((t, DA_WIDTH), BF16),
        jax.ShapeDtypeStruct((t, D_MODEL), BF16),
        jax.ShapeDtypeStruct((t, D_MODEL), BF16),
    ]
    out_specs = [row(HG_WIDTH)] * 5 + [row(DA_WIDTH)] * 3 + [row(D_MODEL)] * 2
    return pl.pallas_call(
        _in_proj_kernel,
        grid=(t // tm,),
        in_specs=[row(D_MODEL), const((1, D_MODEL)), const((D_MODEL, IN_COLS)), const((2, HG_WIDTH)),
                  const((1, DA_WIDTH)), const((1, DA_WIDTH)), tab, tab],
        out_specs=out_specs,
        out_shape=out_shape,
        compiler_params=_cparams(("parallel",)),
        name="in_proj",
    )(x2, norm_mix, w_in_bf, hg_lb, gq, gk, cos_t, sin_t)


def _hgrn_kernel(q_ref, g_ref, k_ref, v_ref, og_ref, gain_ref, o_ref, st_ref):
    @pl.when(pl.program_id(1) == 0)
    def _():
        st_ref[...] = jnp.zeros_like(st_ref)

    c = HG_CHUNK
    row = lax.broadcasted_iota(jnp.int32, (c, c), 0)
    col = lax.broadcasted_iota(jnp.int32, (c, c), 1)
    causal = col <= row
    tri = jnp.where(causal, 1.0, 0.0).astype(BF16)
    gain = gain_ref[...]
    nt = (((1,), (1,)), ((), ()))
    tn = (((0,), (0,)), ((), ()))

    def chunk(ci, carry):
        rows = pl.ds(pl.multiple_of(ci * c, c), c)
        for h in range(HG_HEADS):
            cols = slice(h * HG_DK, (h + 1) * HG_DK)
            gc = g_ref[rows, cols]
            g_hi = gc.astype(BF16)
            g_lo = (gc - g_hi.astype(F32)).astype(BF16)
            b = (jnp.dot(tri, g_hi, preferred_element_type=F32)
                 + jnp.dot(tri, g_lo, preferred_element_type=F32))
            b_mid = b[c // 2 - 1:c // 2, :]
            b_end = b[c - 1:c, :]
            qc = q_ref[rows, cols].astype(F32)
            kc = k_ref[rows, cols].astype(F32)
            vc = v_ref[rows, cols]
            q_rel = (qc * jnp.exp(b - b_mid)).astype(BF16)
            k_rel = (kc * jnp.exp(b_mid - b)).astype(BF16)
            sc = lax.dot_general(q_rel, k_rel, nt, preferred_element_type=F32)
            sc = jnp.where(causal, sc, 0.0).astype(BF16)
            q_st = (qc * jnp.exp(b)).astype(BF16)
            st = st_ref[h]
            o = (jnp.dot(sc, vc, preferred_element_type=F32)
                 + lax.dot_general(q_st, st.astype(BF16), nt, preferred_element_type=F32))
            k_end = (kc * jnp.exp(b_end - b)).astype(BF16)
            st_ref[h] = st * jnp.exp(b_end) + lax.dot_general(vc, k_end, tn, preferred_element_type=F32)
            on = o * lax.rsqrt(jnp.mean(o * o, axis=-1, keepdims=True) + EPS) * gain
            o_ref[rows, cols] = (on * og_ref[rows, cols].astype(F32)).astype(BF16)
        return carry

    lax.fori_loop(0, q_ref.shape[0] // c, chunk, 0, unroll=HGRN_UNROLL)


def _hgrn(hq, g, kk, hi, og, gain, batch, seq):
    t = hq.shape[0]
    tc = TC_HGRN
    ns = seq // tc
    spec = pl.BlockSpec((tc, HG_WIDTH), lambda b, s: (b * ns + s, 0))
    return pl.pallas_call(
        _hgrn_kernel,
        grid=(batch, ns),
        in_specs=[spec] * 5 + [pl.BlockSpec((1, HG_DK), lambda b, s: (0, 0))],
        out_specs=spec,
        out_shape=jax.ShapeDtypeStruct((t, HG_WIDTH), BF16),
        scratch_shapes=[pltpu.VMEM((HG_HEADS, HG_DK, HG_DK), F32)],
        compiler_params=_cparams(("parallel", "arbitrary")),
        name="hgrn",
    )(hq, g, kk, hi, og, gain)


def _attn_kernel(lam_ref, q_ref, k_ref, v_ref, gain_ref, o_ref, s_ref, m_ref, acc_ref):
    tq = q_ref.shape[0]
    dv = DA_VDIM
    i = pl.program_id(2)
    q = q_ref[...]
    lane = lax.broadcasted_iota(jnp.int32, q.shape, 1)
    zero = jnp.zeros_like(q)
    qq = jnp.concatenate([jnp.where(lane < DA_HEAD, q, zero), jnp.where(lane >= DA_HEAD, q, zero)], axis=0)
    m_ref[...] = jnp.full_like(m_ref, NEG)
    acc_ref[...] = jnp.zeros_like(acc_ref)
    nt = (((1,), (1,)), ((), ()))
    ones = jnp.ones((tq, dv), BF16)

    def block_rows(j):
        return pl.ds(pl.multiple_of(j * tq, tq), tq)

    def scores(j):
        return lax.dot_general(qq, k_ref[block_rows(j), :], nt, preferred_element_type=F32)

    def accumulate(slot, j):
        v1 = jnp.concatenate([v_ref[block_rows(j), :], ones], axis=1)
        s = s_ref[slot]
        m_prev = m_ref[...]
        m_new = jnp.maximum(m_prev, jnp.max(s, axis=1, keepdims=True))
        alpha = jnp.exp(m_prev - m_new)
        p = jnp.exp(s - jnp.tile(m_new, (1, tq // LANES)))
        acc_ref[...] = (jnp.tile(alpha, (1, 2)) * acc_ref[...]
                        + jnp.dot(p.astype(BF16), v1, preferred_element_type=F32))
        m_ref[...] = m_new

    r = lax.broadcasted_iota(jnp.int32, (2 * tq, tq), 0)
    c = lax.broadcasted_iota(jnp.int32, (2 * tq, tq), 1)
    s_ref[0] = jnp.where(c <= jnp.where(r >= tq, r - tq, r), scores(i), NEG)

    def pair(m, carry):
        s_ref[1] = scores(2 * m)
        accumulate(0, jnp.where(m == 0, i, 2 * m - 1))
        s_ref[0] = scores(2 * m + 1)
        accumulate(1, 2 * m)
        return carry

    lax.fori_loop(0, i // 2, pair, 0)

    @pl.when(i % 2 == 1)
    def _():
        s_ref[1] = scores(i - 1)
        accumulate(0, jnp.where(i == 1, i, i - 2))

    accumulate(i % 2, jnp.where(i == 0, i, i - 1))

    lp = lam_ref[...]
    lam = (jnp.exp(jnp.sum(lp[0:1] * lp[1:2], axis=1, keepdims=True))
           - jnp.exp(jnp.sum(lp[2:3] * lp[3:4], axis=1, keepdims=True)) + LAM_INIT)
    acc = acc_ref[...]
    o = acc[:tq, :dv] / acc[:tq, dv:] - lam * (acc[tq:, :dv] / acc[tq:, dv:])
    on = o * lax.rsqrt(jnp.mean(o * o, axis=-1, keepdims=True) + EPS) * gain_ref[...] * (1.0 - LAM_INIT)
    o_ref[...] = on.astype(BF16)


def _attn(lam_p, qn, kn, dv, gain, batch, seq):
    t = qn.shape[0]
    tq = TQ_ATTN
    nq = seq // tq
    return pl.pallas_call(
        _attn_kernel,
        grid=(batch, DA_HEADS, nq),
        in_specs=[pl.BlockSpec((4, DA_HEAD), lambda b, h, i: (0, 0)),
                  pl.BlockSpec((tq, DA_VDIM), lambda b, h, i: (b * nq + i, h)),
                  pl.BlockSpec((seq, DA_VDIM), lambda b, h, i: (b, h)),
                  pl.BlockSpec((seq, DA_VDIM), lambda b, h, i: (b, h)),
                  pl.BlockSpec((1, DA_VDIM), lambda b, h, i: (0, 0))],
        out_specs=pl.BlockSpec((tq, DA_VDIM), lambda b, h, i: (b * nq + i, h)),
        out_shape=jax.ShapeDtypeStruct((t, DA_WIDTH), BF16),
        scratch_shapes=[pltpu.VMEM((2, 2 * tq, tq), F32), pltpu.VMEM((2 * tq, LANES), F32),
                        pltpu.VMEM((2 * tq, 2 * DA_VDIM), F32)],
        compiler_params=_cparams(("parallel", "parallel", "arbitrary")),
        name="attn",
    )(lam_p, qn, kn, dv, gain)


def _merge_kernel(ohg_ref, oda_ref, sgh_ref, sgd_ref, x_ref, wbh_ref, wbd_ref, wo_ref, nm_ref, wr_ref, br_ref,
                  xmid_ref, xn_ref, meta_ref, cnt_ref, carry_ref):
    tm = x_ref.shape[0]

    @pl.when(pl.program_id(0) == 0)
    def _():
        carry_ref[...] = jnp.zeros_like(carry_ref)

    y_hg = jnp.dot(ohg_ref[...], wbh_ref[...], preferred_element_type=F32)
    y_da = jnp.dot(oda_ref[...], wbd_ref[...], preferred_element_type=F32)
    mixed = sgh_ref[...].astype(F32) * y_hg + sgd_ref[...].astype(F32) * y_da
    xm = x_ref[...] + jnp.dot(mixed.astype(BF16), wo_ref[...], preferred_element_type=F32)
    xmid_ref[...] = xm
    xn = xm * lax.rsqrt(jnp.mean(xm * xm, axis=-1, keepdims=True) + EPS) * nm_ref[...]
    for c in range(ROW_TILES):
        xn_ref[pl.ds(c, tm, stride=ROW_TILES), :] = xn[:, c * LANES:(c + 1) * LANES]

    w = wr_ref[...]
    w_hi = w.astype(BF16)
    w_lo = (w - w_hi.astype(F32)).astype(BF16)
    x_hi = xn.astype(BF16)
    x_lo = (xn - x_hi.astype(F32)).astype(BF16)
    logit = (jnp.dot(x_hi, w_hi, preferred_element_type=F32) + jnp.dot(x_hi, w_lo, preferred_element_type=F32)
             + jnp.dot(x_lo, w_hi, preferred_element_type=F32) + br_ref[...])

    lane = lax.broadcasted_iota(jnp.int32, (tm, LANES), 1)
    lanef = lane.astype(F32)
    big = float(LANES)
    is_g = lane < N_GROUPS
    gl = jnp.where(is_g, logit, NEG)
    gmax = jnp.max(gl, axis=1, keepdims=True)
    g_sel = jnp.min(jnp.where(gl == gmax, lanef, big), axis=1, keepdims=True)
    g_w = 1.0 / jnp.sum(jnp.where(is_g, jnp.exp(gl - gmax), 0.0), axis=1, keepdims=True)
    lo = N_GROUPS + EXPERTS_PER_GROUP * g_sel
    in_grp = (lanef >= lo) & (lanef < lo + EXPERTS_PER_GROUP)
    el = jnp.where(in_grp, logit, NEG)
    v1 = jnp.max(el, axis=1, keepdims=True)
    i1 = jnp.min(jnp.where(el == v1, lanef, big), axis=1, keepdims=True)
    el2 = jnp.where(lanef == i1, NEG, el)
    v2 = jnp.max(el2, axis=1, keepdims=True)
    i2 = jnp.min(jnp.where(el2 == v2, lanef, big), axis=1, keepdims=True)
    e1 = i1 - N_GROUPS
    e2 = i2 - N_GROUPS
    d = jnp.exp(v2 - v1)
    w1 = g_w / (1.0 + d)
    w2 = g_w * d / (1.0 + d)

    oh1 = jnp.where(lanef == e1, 1.0, 0.0)
    oh2 = jnp.where(lanef == e2, 1.0, 0.0)
    r = lax.broadcasted_iota(jnp.int32, (tm, tm), 0)
    c = lax.broadcasted_iota(jnp.int32, (tm, tm), 1)
    below = jnp.where(c < r, 1.0, 0.0).astype(BF16)
    c1 = jnp.dot(below, oh1.astype(BF16), preferred_element_type=F32)
    c2 = jnp.dot(below, oh2.astype(BF16), preferred_element_type=F32)
    tot1 = jnp.sum(oh1, axis=0, keepdims=True)
    tot2 = jnp.sum(oh2, axis=0, keepdims=True)
    carry = carry_ref[...]
    rank1 = jnp.sum(oh1 * (c1 + carry), axis=1, keepdims=True)
    rank2 = jnp.sum(oh2 * (c2 + carry + tot1), axis=1, keepdims=True)
    carry = carry + tot1 + tot2
    carry_ref[...] = carry
    cnt_ref[...] = jnp.broadcast_to(carry, cnt_ref.shape)

    meta = jnp.where(lane == 0, e1, jnp.where(lane == 1, e2, jnp.where(lane == 2, rank1, jnp.where(
        lane == 3, rank2, jnp.where(lane == 4, w1, jnp.where(lane == 5, w2, 0.0))))))
    meta_ref[...] = meta


def _merge(o_hg, o_da, sgh, sgd, x2, wbh, wbd, wo, norm_moe, wr, br):
    t = x2.shape[0]
    tm = TM_MERGE
    row = lambda w: pl.BlockSpec((tm, w), lambda i: (i, 0))
    const = lambda shape: pl.BlockSpec(shape, lambda i: (0, 0))
    return pl.pallas_call(
        _merge_kernel,
        grid=(t // tm,),
        in_specs=[row(HG_WIDTH), row(DA_WIDTH), row(D_MODEL), row(D_MODEL), row(D_MODEL),
                  const((HG_WIDTH, D_MODEL)), const((DA_WIDTH, D_MODEL)), const((D_MODEL, D_MODEL)),
                  const((1, D_MODEL)), const((D_MODEL, LANES)), const((1, LANES))],
        out_specs=[row(D_MODEL), pl.BlockSpec((tm * ROW_TILES, LANES), lambda i: (i, 0)), row(LANES),
                   const((SUBLANES, LANES))],
        out_shape=[jax.ShapeDtypeStruct((t, D_MODEL), F32),
                   jax.ShapeDtypeStruct((t * ROW_TILES, LANES), F32),
                   jax.ShapeDtypeStruct((t, LANES), F32),
                   jax.ShapeDtypeStruct((SUBLANES, LANES), F32)],
        scratch_shapes=[pltpu.VMEM((1, LANES), F32)],
        compiler_params=_cparams(("arbitrary",)),
        name="merge",
    )(o_hg, o_da, sgh, sgd, x2, wbh, wbd, wo, norm_moe, wr, br)


def _dispatch_kernel(pos_hbm, xn_ref, xg_in, xg_out, pos_smem, sem_p, sem_o):
    del xg_in
    tm = xn_ref.shape[0]
    i = pl.program_id(0)
    cp = pltpu.make_async_copy(pos_hbm.at[i], pos_smem, sem_p)
    cp.start()
    cp.wait()

    def row_copy(r, k):
        return pltpu.make_async_copy(xn_ref.at[r], xg_out.at[pos_smem[k * tm + r]], sem_o)

    def start(r, carry):
        row_copy(r, 0).start()
        row_copy(r, 1).start()
        return carry

    def wait(r, carry):
        row_copy(r, 0).wait()
        row_copy(r, 1).wait()
        return carry

    lax.fori_loop(0, tm, start, 0)
    lax.fori_loop(0, tm, wait, 0)


def _dispatch(pos_tiles, xn3, n_rows):
    t = xn3.shape[0]
    tm = TM_MOVE
    xg0 = jnp.zeros((n_rows, ROW_TILES, LANES), F32)
    return pl.pallas_call(
        _dispatch_kernel,
        grid=(t // tm,),
        in_specs=[pl.BlockSpec(memory_space=pl.ANY),
                  pl.BlockSpec((tm, ROW_TILES, LANES), lambda i: (i, 0, 0)),
                  pl.BlockSpec(memory_space=pl.ANY)],
        out_specs=pl.BlockSpec(memory_space=pl.ANY),
        out_shape=jax.ShapeDtypeStruct((n_rows, ROW_TILES, LANES), F32),
        scratch_shapes=[pltpu.SMEM((TOP_K * tm,), jnp.int32), pltpu.SemaphoreType.DMA, pltpu.SemaphoreType.DMA],
        input_output_aliases={2: 0},
        compiler_params=_cparams(("arbitrary",)),
        name="dispatch",
    )(pos_tiles, xn3, xg0)


def _expert_kernel(be_ref, xg_ref, w1_ref, w3_ref, w2_ref, y_ref):
    del be_ref
    tm = xg_ref.shape[0] // ROW_TILES
    xb = jnp.concatenate([xg_ref[pl.ds(c, tm, stride=ROW_TILES), :] for c in range(ROW_TILES)], axis=1).astype(BF16)
    a = jnp.dot(xb, w1_ref[0], preferred_element_type=F32)
    b = jnp.dot(xb, w3_ref[0], preferred_element_type=F32)
    hid = (a * _sigmoid(a) * b).astype(BF16)
    y = jnp.dot(hid, w2_ref[0], preferred_element_type=F32)
    for c in range(ROW_TILES):
        y_ref[pl.ds(c, tm, stride=ROW_TILES), :] = y[:, c * LANES:(c + 1) * LANES]


def _experts(block_expert, xg2, w1, w3, w2):
    tm = TM_EXPERT
    nb = xg2.shape[0] // (tm * ROW_TILES)
    grid_spec = pltpu.PrefetchScalarGridSpec(
        num_scalar_prefetch=1,
        grid=(nb,),
        in_specs=[pl.BlockSpec((tm * ROW_TILES, LANES), lambda i, be: (i, 0)),
                  pl.BlockSpec((1, D_MODEL, D_FF), lambda i, be: (be[i], 0, 0)),
                  pl.BlockSpec((1, D_MODEL, D_FF), lambda i, be: (be[i], 0, 0)),
                  pl.BlockSpec((1, D_FF, D_MODEL), lambda i, be: (be[i], 0, 0))],
        out_specs=pl.BlockSpec((tm * ROW_TILES, LANES), lambda i, be: (i, 0)),
    )
    return pl.pallas_call(
        _expert_kernel,
        grid_spec=grid_spec,
        out_shape=jax.ShapeDtypeStruct(xg2.shape, F32),
        compiler_params=_cparams(("arbitrary",)),
        name="experts",
    )(block_expert, xg2, w1, w3, w2)


def _combine_kernel(pos_hbm, meta_ref, xmid_ref, y_hbm, o_ref, pos_smem, buf_ref, sem_p, sem_i):
    tm = xmid_ref.shape[0]
    i = pl.program_id(0)
    cp = pltpu.make_async_copy(pos_hbm.at[i], pos_smem, sem_p)
    cp.start()
    cp.wait()

    def row_copy(r, k):
        dst = buf_ref.at[k, pl.ds(pl.multiple_of(r * ROW_TILES, ROW_TILES), ROW_TILES), :]
        return pltpu.make_async_copy(y_hbm.at[pos_smem[k * tm + r]], dst, sem_i)

    def start(r, carry):
        row_copy(r, 0).start()
        row_copy(r, 1).start()
        return carry

    def wait(r, carry):
        row_copy(r, 0).wait()
        row_copy(r, 1).wait()
        return carry

    lax.fori_loop(0, tm, start, 0)
    lax.fori_loop(0, tm, wait, 0)

    meta = meta_ref[...]
    w1 = meta[:, 4:5]
    w2 = meta[:, 5:6]
    for c in range(ROW_TILES):
        cols = slice(c * LANES, (c + 1) * LANES)
        y1 = buf_ref[0, pl.ds(c, tm, stride=ROW_TILES), :]
        y2 = buf_ref[1, pl.ds(c, tm, stride=ROW_TILES), :]
        o_ref[:, cols] = xmid_ref[:, cols] + (w1 * y1 + w2 * y2)


def _combine(pos_tiles, meta, xmid, y3):
    t = xmid.shape[0]
    tm = TM_MOVE
    return pl.pallas_call(
        _combine_kernel,
        grid=(t // tm,),
        in_specs=[pl.BlockSpec(memory_space=pl.ANY),
                  pl.BlockSpec((tm, LANES), lambda i: (i, 0)),
                  pl.BlockSpec((tm, D_MODEL), lambda i: (i, 0)),
                  pl.BlockSpec(memory_space=pl.ANY)],
        out_specs=pl.BlockSpec((tm, D_MODEL), lambda i: (i, 0)),
        out_shape=jax.ShapeDtypeStruct((t, D_MODEL), F32),
        scratch_shapes=[pltpu.SMEM((TOP_K * tm,), jnp.int32),
                        pltpu.VMEM((TOP_K, tm * ROW_TILES, LANES), F32),
                        pltpu.SemaphoreType.DMA, pltpu.SemaphoreType.DMA],
        compiler_params=_cparams(("arbitrary",)),
        name="combine",
    )(pos_tiles, meta, xmid, y3)


def kernel(x, norm_mix, w_in, hg_lb, hg_out_norm, da_q_norm, da_k_norm, da_lambda, da_out_norm, w_branch_hg,
           w_branch_da, w_out, norm_moe, w_router_group, b_router_group, w_router_expert, b_router_expert,
           w1, w3, w2):
    batch, seq, d = x.shape
    assert d == D_MODEL and norm_mix.shape[0] == 1 and w_in.shape[2] == IN_COLS
    assert seq % TC_HGRN == 0 and seq % TQ_ATTN == 0 and (batch * seq) % TM_MOVE == 0
    t = batch * seq
    x2 = x.reshape(t, d)

    half = DA_HEAD // 2
    inv = ROPE_THETA ** (-jnp.arange(half, dtype=F32) / half)
    ang = jnp.arange(seq, dtype=F32)[:, None] * inv[None, :]
    cos_t = jnp.tile(jnp.cos(ang), (1, 2 * LANES // DA_HEAD))
    sin_t = jnp.tile(jnp.concatenate([-jnp.sin(ang), jnp.sin(ang)], axis=1), (1, LANES // DA_HEAD))

    reps = DA_WIDTH // DA_HEAD
    gq = jnp.tile(da_q_norm[0].astype(F32) * (DA_HEAD ** -0.5), reps)[None, :]
    gk = jnp.tile(da_k_norm[0].astype(F32), reps)[None, :]

    hq, g, kk, hi, og, qn, kn, dv, sgh, sgd = _in_proj(
        x2, norm_mix.astype(F32), w_in[0].astype(BF16), hg_lb.astype(F32), gq, gk, cos_t, sin_t, seq)

    o_hg = _hgrn(hq, g, kk, hi, og, hg_out_norm.astype(F32), batch, seq)
    o_da = _attn(da_lambda[0].astype(F32), qn, kn, dv, da_out_norm.astype(F32), batch, seq)

    wr = jnp.zeros((D_MODEL, LANES), F32)
    wr = wr.at[:, :N_GROUPS].set(w_router_group[0]).at[:, N_GROUPS:N_GROUPS + N_EXPERTS].set(w_router_expert[0])
    br = jnp.zeros((1, LANES), F32)
    br = br.at[0, :N_GROUPS].set(b_router_group[0]).at[0, N_GROUPS:N_GROUPS + N_EXPERTS].set(b_router_expert[0])

    xmid, xn_rm, meta, cnt = _merge(o_hg, o_da, sgh, sgd, x2, w_branch_hg[0].astype(BF16),
                                    w_branch_da[0].astype(BF16), w_out[0].astype(BF16), norm_moe.astype(F32), wr, br)

    tmx = TM_EXPERT
    counts = cnt[0, :N_EXPERTS].astype(jnp.int32)
    padded = (counts + tmx - 1) // tmx * tmx
    pad_end = jnp.cumsum(padded)
    pad_start = pad_end - padded
    eid = meta[:, 0:TOP_K].astype(jnp.int32)
    rank = meta[:, TOP_K:2 * TOP_K].astype(jnp.int32)
    pos = pad_start[eid] + rank
    n_rows = t * TOP_K + N_EXPERTS * tmx
    nb = n_rows // tmx
    block_start = jnp.arange(nb, dtype=jnp.int32) * tmx
    block_expert = jnp.minimum(jnp.sum(pad_end[None, :] <= block_start[:, None], axis=1),
                               N_EXPERTS - 1).astype(jnp.int32)
    nt = t // TM_MOVE
    pos_tiles = pos.T.reshape(TOP_K, nt, TM_MOVE).transpose(1, 0, 2).reshape(nt, TOP_K * TM_MOVE)

    xg = _dispatch(pos_tiles, xn_rm.reshape(t, ROW_TILES, LANES), n_rows)
    y = _experts(block_expert, xg.reshape(n_rows * ROW_TILES, LANES), w1[0].astype(BF16), w3[0].astype(BF16),
                 w2[0].astype(BF16))
    out = _combine(pos_tiles, meta, xmid, y.reshape(n_rows, ROW_TILES, LANES))
    return out.reshape(batch, seq, d)
```

```python
import functools
import math

import jax
import jax.numpy as jnp
from jax import lax
from jax.experimental import pallas as pl
from jax.experimental.pallas import tpu as pltpu

F32 = jnp.float32
BF16 = jnp.bfloat16

D_MODEL = 1024
HG_HEADS = 4
HG_DK = 128
HG_WIDTH = HG_HEADS * HG_DK
HG_CHUNK = 64
DA_HEADS = 4
DA_HEAD = 64
DA_VDIM = 2 * DA_HEAD
DA_WIDTH = DA_HEADS * DA_VDIM
ROPE_THETA = 10000.0
N_GROUPS = 4
EXPERTS_PER_GROUP = 8
N_EXPERTS = N_GROUPS * EXPERTS_PER_GROUP
TOP_K = 2
D_FF = 512
EPS = 1e-6
LAM_INIT = 0.8 - 0.6 * math.exp(-0.3 * 0)
IN_COLS = 4 * HG_WIDTH + 3 * DA_WIDTH + 2 * D_MODEL

LANES = 128
SUBLANES = 8
BF16_ROWS = 16
NEG = -1e30

TM_PROJ = 256
TC_HGRN = 512
HGRN_UNROLL = 4
TQ_ATTN = 512
TM_MOE = 512
TM_EXPERT = 256
MOE_CHUNK = BF16_ROWS
MOE_LOCAL = TOP_K * TM_MOE + N_EXPERTS * MOE_CHUNK
TAB_GSTART, TAB_LSTART, TAB_NCHUNK = 0, N_EXPERTS, 2 * N_EXPERTS
VMEM_LIMIT = 56 * 1024 * 1024


def _cparams(sem):
    return pltpu.CompilerParams(dimension_semantics=sem, vmem_limit_bytes=VMEM_LIMIT)


def _sigmoid(v):
    return 1.0 / (1.0 + jnp.exp(-v))


def _qk_prep(d, gain, cos, sin, gmat):
    ss = jnp.dot((d * d).astype(BF16), gmat, preferred_element_type=F32)
    y = d * lax.rsqrt(ss * (1.0 / DA_HEAD) + EPS) * gain
    lane = lax.broadcasted_iota(jnp.int32, (d.shape[0], LANES), 1)
    upper = (lane & (DA_HEAD // 2)) != 0
    outs = []
    for c in range(d.shape[1] // LANES):
        yc = y[:, c * LANES:(c + 1) * LANES]
        sw = jnp.where(upper, pltpu.roll(yc, DA_HEAD // 2, 1), pltpu.roll(yc, LANES - DA_HEAD // 2, 1))
        outs.append(yc * cos + sw * sin)
    return jnp.concatenate(outs, axis=1)


def _in_proj_kernel(x_ref, nm_ref, w_ref, lb_ref, gq_ref, gk_ref, cos_ref, sin_ref,
                    hq_ref, g_ref, kk_ref, hi_ref, og_ref, qn_ref, kn_ref, dv_ref, sgh_ref, sgd_ref):
    x = x_ref[...]
    h = x * lax.rsqrt(jnp.mean(x * x, axis=-1, keepdims=True) + EPS) * nm_ref[...]
    hb = h.astype(BF16)

    def proj(c0, width):
        return jnp.dot(hb, w_ref[:, c0:c0 + width], preferred_element_type=F32)

    hq_ref[...] = proj(0, HG_WIDTH).astype(BF16)

    hf = proj(HG_WIDTH, HG_WIDTH)
    lbp = lb_ref[...]
    mx = jnp.maximum(lbp[0:1], lbp[1:2])
    e0 = jnp.exp(lbp[0:1] - mx)
    e1 = jnp.exp(lbp[1:2] - mx)
    lb = e0 / (e0 + e1)
    f = lb + (1.0 - lb) * _sigmoid(hf)
    g_ref[...] = jnp.log(f)
    kk_ref[...] = (1.0 - f).astype(BF16)

    hi_ref[...] = proj(2 * HG_WIDTH, HG_WIDTH).astype(BF16)
    hog = proj(3 * HG_WIDTH, HG_WIDTH)
    og_ref[...] = (hog * _sigmoid(hog)).astype(BF16)

    r = lax.broadcasted_iota(jnp.int32, (DA_WIDTH, DA_WIDTH), 0) // DA_HEAD
    c = lax.broadcasted_iota(jnp.int32, (DA_WIDTH, DA_WIDTH), 1) // DA_HEAD
    gmat = jnp.where(r == c, 1.0, 0.0).astype(BF16)
    cos = cos_ref[...]
    sin = sin_ref[...]
    base = 4 * HG_WIDTH
    qn_ref[...] = _qk_prep(proj(base, DA_WIDTH), gq_ref[...], cos, sin, gmat).astype(BF16)
    kn_ref[...] = _qk_prep(proj(base + DA_WIDTH, DA_WIDTH), gk_ref[...], cos, sin, gmat).astype(BF16)
    dv_ref[...] = proj(base + 2 * DA_WIDTH, DA_WIDTH).astype(BF16)
    base += 3 * DA_WIDTH
    sgh_ref[...] = _sigmoid(proj(base, D_MODEL)).astype(BF16)
    sgd_ref[...] = _sigmoid(proj(base + D_MODEL, D_MODEL)).astype(BF16)


def _in_proj(x2, norm_mix, w_in_bf, hg_lb, gq, gk, cos_t, sin_t, seq):
    t = x2.shape[0]
    tm = TM_PROJ
    nseq = seq // tm
    row = lambda w: pl.BlockSpec((tm, w), lambda i: (i, 0))
    const = lambda shape: pl.BlockSpec(shape, lambda i: (0, 0))
    tab = pl.BlockSpec((tm, LANES), lambda i: (i % nseq, 0))
    out_shape = [
        jax.ShapeDtypeStruct((t, HG_WIDTH), BF16),
        jax.ShapeDtypeStruct((t, HG_WIDTH), F32),
        jax.ShapeDtypeStruct((t, HG_WIDTH), BF16),
        jax.ShapeDtypeStruct((t, HG_WIDTH), BF16),
        jax.ShapeDtypeStruct((t, HG_WIDTH), BF16),
        jax.ShapeDtypeStruct((t, DA_WIDTH), BF16),
        jax.ShapeDtypeStruct((t, DA_WIDTH), BF16),
        jax.ShapeDtypeStruct((t, DA_WIDTH), BF16),
        jax.ShapeDtypeStruct((t, D_MODEL), BF16),
        jax.ShapeDtypeStruct((t, D_MODEL), BF16),
    ]
    out_specs = [row(HG_WIDTH)] * 5 + [row(DA_WIDTH)] * 3 + [row(D_MODEL)] * 2
    return pl.pallas_call(
        _in_proj_kernel,
        grid=(t // tm,),
        in_specs=[row(D_MODEL), const((1, D_MODEL)), const((D_MODEL, IN_COLS)), const((2, HG_WIDTH)),
                  const((1, DA_WIDTH)), const((1, DA_WIDTH)), tab, tab],
        out_specs=out_specs,
        out_shape=out_shape,
        compiler_params=_cparams(("parallel",)),
        name="in_proj",
    )(x2, norm_mix, w_in_bf, hg_lb, gq, gk, cos_t, sin_t)


def _hgrn_kernel(q_ref, g_ref, k_ref, v_ref, og_ref, gain_ref, o_ref, st_ref):
    @pl.when(pl.program_id(1) == 0)
    def _():
        st_ref[...] = jnp.zeros_like(st_ref)

    c = HG_CHUNK
    row = lax.broadcasted_iota(jnp.int32, (c, c), 0)
    col = lax.broadcasted_iota(jnp.int32, (c, c), 1)
    causal = col <= row
    tri = jnp.where(causal, 1.0, 0.0).astype(BF16)
    gain = gain_ref[...]
    nt = (((1,), (1,)), ((), ()))
    tn = (((0,), (0,)), ((), ()))

    def chunk(ci, carry):
        rows = pl.ds(pl.multiple_of(ci * c, c), c)
        for h in range(HG_HEADS):
            cols = slice(h * HG_DK, (h + 1) * HG_DK)
            gc = g_ref[rows, cols]
            g_hi = gc.astype(BF16)
            g_lo = (gc - g_hi.astype(F32)).astype(BF16)
            b = (jnp.dot(tri, g_hi, preferred_element_type=F32)
                 + jnp.dot(tri, g_lo, preferred_element_type=F32))
            b_mid = b[c // 2 - 1:c // 2, :]
            b_end = b[c - 1:c, :]
            qc = q_ref[rows, cols].astype(F32)
            kc = k_ref[rows, cols].astype(F32)
            vc = v_ref[rows, cols]
            q_rel = (qc * jnp.exp(b - b_mid)).astype(BF16)
            k_rel = (kc * jnp.exp(b_mid - b)).astype(BF16)
            sc = lax.dot_general(q_rel, k_rel, nt, preferred_element_type=F32)
            sc = jnp.where(causal, sc, 0.0).astype(BF16)
            q_st = (qc * jnp.exp(b)).astype(BF16)
            st = st_ref[h]
            o = (jnp.dot(sc, vc, preferred_element_type=F32)
                 + lax.dot_general(q_st, st.astype(BF16), nt, preferred_element_type=F32))
            k_end = (kc * jnp.exp(b_end - b)).astype(BF16)
            st_ref[h] = st * jnp.exp(b_end) + lax.dot_general(vc, k_end, tn, preferred_element_type=F32)
            on = o * lax.rsqrt(jnp.mean(o * o, axis=-1, keepdims=True) + EPS) * gain
            o_ref[rows, cols] = (on * og_ref[rows, cols].astype(F32)).astype(BF16)
        return carry

    lax.fori_loop(0, q_ref.shape[0] // c, chunk, 0, unroll=HGRN_UNROLL)


def _hgrn(hq, g, kk, hi, og, gain, batch, seq):
    t = hq.shape[0]
    tc = TC_HGRN
    ns = seq // tc
    spec = pl.BlockSpec((tc, HG_WIDTH), lambda b, s: (b * ns + s, 0))
    return pl.pallas_call(
        _hgrn_kernel,
        grid=(batch, ns),
        in_specs=[spec] * 5 + [pl.BlockSpec((1, HG_DK), lambda b, s: (0, 0))],
        out_specs=spec,
        out_shape=jax.ShapeDtypeStruct((t, HG_WIDTH), BF16),
        scratch_shapes=[pltpu.VMEM((HG_HEADS, HG_DK, HG_DK), F32)],
        compiler_params=_cparams(("parallel", "arbitrary")),
        name="hgrn",
    )(hq, g, kk, hi, og, gain)


def _attn_kernel(lam_ref, q_ref, k_ref, v_ref, gain_ref, o_ref, s_ref, m_ref, acc_ref):
    tq = q_ref.shape[0]
    dv = DA_VDIM
    i = pl.program_id(2)
    q = q_ref[...]
    lane = lax.broadcasted_iota(jnp.int32, q.shape, 1)
    zero = jnp.zeros_like(q)
    qq = jnp.concatenate([jnp.where(lane < DA_HEAD, q, zero), jnp.where(lane >= DA_HEAD, q, zero)], axis=0)
    m_ref[...] = jnp.full_like(m_ref, NEG)
    acc_ref[...] = jnp.zeros_like(acc_ref)
    nt = (((1,), (1,)), ((), ()))
    ones = jnp.ones((tq, dv), BF16)

    def block_rows(j):
        return pl.ds(pl.multiple_of(j * tq, tq), tq)

    def scores(j):
        return lax.dot_general(qq, k_ref[block_rows(j), :], nt, preferred_element_type=F32)

    def accumulate(slot, j):
        v1 = jnp.concatenate([v_ref[block_rows(j), :], ones], axis=1)
        s = s_ref[slot]
        m_prev = m_ref[...]
        m_new = jnp.maximum(m_prev, jnp.max(s, axis=1, keepdims=True))
        alpha = jnp.exp2(m_prev - m_new)
        p = jnp.exp2(s - jnp.tile(m_new, (1, tq // LANES)))
        acc_ref[...] = (jnp.tile(alpha, (1, 2)) * acc_ref[...]
                        + jnp.dot(p.astype(BF16), v1, preferred_element_type=F32))
        m_ref[...] = m_new

    r = lax.broadcasted_iota(jnp.int32, (2 * tq, tq), 0)
    c = lax.broadcasted_iota(jnp.int32, (2 * tq, tq), 1)
    s_ref[0] = jnp.where(c <= jnp.where(r >= tq, r - tq, r), scores(i), NEG)

    def pair(m, carry):
        s_ref[1] = scores(2 * m)
        accumulate(0, jnp.where(m == 0, i, 2 * m - 1))
        s_ref[0] = scores(2 * m + 1)
        accumulate(1, 2 * m)
        return carry

    lax.fori_loop(0, i // 2, pair, 0)

    @pl.when(i % 2 == 1)
    def _():
        s_ref[1] = scores(i - 1)
        accumulate(0, jnp.where(i == 1, i, i - 2))

    accumulate(i % 2, jnp.where(i == 0, i, i - 1))

    lp = lam_ref[...]
    lam = (jnp.exp(jnp.sum(lp[0:1] * lp[1:2], axis=1, keepdims=True))
           - jnp.exp(jnp.sum(lp[2:3] * lp[3:4], axis=1, keepdims=True)) + LAM_INIT)
    acc = acc_ref[...]
    o = acc[:tq, :dv] / acc[:tq, dv:] - lam * (acc[tq:, :dv] / acc[tq:, dv:])
    on = o * lax.rsqrt(jnp.mean(o * o, axis=-1, keepdims=True) + EPS) * gain_ref[...] * (1.0 - LAM_INIT)
    o_ref[...] = on.astype(BF16)


def _attn(lam_p, qn, kn, dv, gain, batch, seq):
    t = qn.shape[0]
    tq = TQ_ATTN
    nq = seq // tq
    return pl.pallas_call(
        _attn_kernel,
        grid=(batch, DA_HEADS, nq),
        in_specs=[pl.BlockSpec((4, DA_HEAD), lambda b, h, i: (0, 0)),
                  pl.BlockSpec((tq, DA_VDIM), lambda b, h, i: (b * nq + i, h)),
                  pl.BlockSpec((seq, DA_VDIM), lambda b, h, i: (b, h)),
                  pl.BlockSpec((seq, DA_VDIM), lambda b, h, i: (b, h)),
                  pl.BlockSpec((1, DA_VDIM), lambda b, h, i: (0, 0))],
        out_specs=pl.BlockSpec((tq, DA_VDIM), lambda b, h, i: (b * nq + i, h)),
        out_shape=jax.ShapeDtypeStruct((t, DA_WIDTH), BF16),
        scratch_shapes=[pltpu.VMEM((2, 2 * tq, tq), F32), pltpu.VMEM((2 * tq, LANES), F32),
                        pltpu.VMEM((2 * tq, 2 * DA_VDIM), F32)],
        compiler_params=_cparams(("parallel", "parallel", "arbitrary")),
        name="attn",
    )(lam_p, qn, kn, dv, gain)


def _merge_kernel(ohg_ref, oda_ref, sgh_ref, sgd_ref, x_ref, wbh_ref, wbd_ref, wo_ref, nm_ref, wr_ref, br_ref,
                  xmid_ref, xn_ref, meta_ref, metat_ref, ttab_ref, cnt_ref, carry_ref):
    tm = x_ref.shape[0]

    @pl.when(pl.program_id(0) == 0)
    def _():
        carry_ref[...] = jnp.zeros_like(carry_ref)

    y_hg = jnp.dot(ohg_ref[...], wbh_ref[...], preferred_element_type=F32)
    y_da = jnp.dot(oda_ref[...], wbd_ref[...], preferred_element_type=F32)
    mixed = sgh_ref[...].astype(F32) * y_hg + sgd_ref[...].astype(F32) * y_da
    xm = x_ref[...] + jnp.dot(mixed.astype(BF16), wo_ref[...], preferred_element_type=F32)
    xmid_ref[...] = xm
    xn = xm * lax.rsqrt(jnp.mean(xm * xm, axis=-1, keepdims=True) + EPS) * nm_ref[...]
    xn_ref[...] = xn.astype(BF16)

    w = wr_ref[...]
    w_hi = w.astype(BF16)
    w_lo = (w - w_hi.astype(F32)).astype(BF16)
    x_hi = xn.astype(BF16)
    x_lo = (xn - x_hi.astype(F32)).astype(BF16)
    logit = (jnp.dot(x_hi, w_hi, preferred_element_type=F32) + jnp.dot(x_hi, w_lo, preferred_element_type=F32)
             + jnp.dot(x_lo, w_hi, preferred_element_type=F32) + br_ref[...])

    lane = lax.broadcasted_iota(jnp.int32, (tm, LANES), 1)
    lanef = lane.astype(F32)
    big = float(LANES)
    is_g = lane < N_GROUPS
    gl = jnp.where(is_g, logit, NEG)
    gmax = jnp.max(gl, axis=1, keepdims=True)
    g_sel = jnp.min(jnp.where(gl == gmax, lanef, big), axis=1, keepdims=True)
    g_w = 1.0 / jnp.sum(jnp.where(is_g, jnp.exp(gl - gmax), 0.0), axis=1, keepdims=True)
    lo = N_GROUPS + EXPERTS_PER_GROUP * g_sel
    in_grp = (lanef >= lo) & (lanef < lo + EXPERTS_PER_GROUP)
    el = jnp.where(in_grp, logit, NEG)
    v1 = jnp.max(el, axis=1, keepdims=True)
    i1 = jnp.min(jnp.where(el == v1, lanef, big), axis=1, keepdims=True)
    el2 = jnp.where(lanef == i1, NEG, el)
    v2 = jnp.max(el2, axis=1, keepdims=True)
    i2 = jnp.min(jnp.where(el2 == v2, lanef, big), axis=1, keepdims=True)
    e1 = i1 - N_GROUPS
    e2 = i2 - N_GROUPS
    d = jnp.exp(v2 - v1)
    w1 = g_w / (1.0 + d)
    w2 = g_w * d / (1.0 + d)

    oh1 = jnp.where(lanef == e1, 1.0, 0.0)
    oh2 = jnp.where(lanef == e2, 1.0, 0.0)
    r = lax.broadcasted_iota(jnp.int32, (tm, tm), 0)
    c = lax.broadcasted_iota(jnp.int32, (tm, tm), 1)
    below = jnp.where(c < r, 1.0, 0.0).astype(BF16)
    c1 = jnp.dot(below, oh1.astype(BF16), preferred_element_type=F32)
    c2 = jnp.dot(below, oh2.astype(BF16), preferred_element_type=F32)
    tot1 = jnp.sum(oh1, axis=0, keepdims=True)
    tot2 = jnp.sum(oh2, axis=0, keepdims=True)
    n_chunk = jnp.floor((tot1 + tot2 + (MOE_CHUNK - 1)) * (1.0 / MOE_CHUNK))
    r = lax.broadcasted_iota(jnp.int32, (LANES, LANES), 0)
    c = lax.broadcasted_iota(jnp.int32, (LANES, LANES), 1)
    before = jnp.where(r < c, 1.0, 0.0).astype(BF16)
    lstart = MOE_CHUNK * jnp.dot(jnp.broadcast_to(n_chunk, (SUBLANES, LANES)).astype(BF16), before,
                                 preferred_element_type=F32)[0:1]
    lo1 = jnp.sum(oh1 * (c1 + lstart), axis=1, keepdims=True)
    lo2 = jnp.sum(oh2 * (c2 + lstart + tot1), axis=1, keepdims=True)
    carry = carry_ref[...]
    ttab_ref[...] = jnp.concatenate([carry, MOE_CHUNK * n_chunk, lstart, jnp.zeros((SUBLANES - 3, LANES), F32)], axis=0)
    carry = carry + MOE_CHUNK * n_chunk
    carry_ref[...] = carry
    cnt_ref[...] = jnp.broadcast_to(carry, cnt_ref.shape)

    meta = jnp.where(lane == 0, e1, jnp.where(lane == 1, e2, jnp.where(lane == 2, lo1, jnp.where(
        lane == 3, lo2, jnp.where(lane == 4, w1, jnp.where(lane == 5, w2, 0.0))))))
    meta_ref[...] = meta
    metat_ref[...] = meta.T[0:SUBLANES, :]


def _merge(o_hg, o_da, sgh, sgd, x2, wbh, wbd, wo, norm_moe, wr, br):
    t = x2.shape[0]
    tm = TM_MOE
    nt = t // tm
    row = lambda w: pl.BlockSpec((tm, w), lambda i: (i, 0))
    const = lambda shape: pl.BlockSpec(shape, lambda i: (0, 0))
    return pl.pallas_call(
        _merge_kernel,
        grid=(nt,),
        in_specs=[row(HG_WIDTH), row(DA_WIDTH), row(D_MODEL), row(D_MODEL), row(D_MODEL),
                  const((HG_WIDTH, D_MODEL)), const((DA_WIDTH, D_MODEL)), const((D_MODEL, D_MODEL)),
                  const((1, D_MODEL)), const((D_MODEL, LANES)), const((1, LANES))],
        out_specs=[row(D_MODEL), row(D_MODEL), row(LANES),
                   pl.BlockSpec((SUBLANES, tm), lambda i: (0, i)),
                   pl.BlockSpec((SUBLANES, LANES), lambda i: (i, 0)),
                   const((SUBLANES, LANES))],
        out_shape=[jax.ShapeDtypeStruct((t, D_MODEL), F32),
                   jax.ShapeDtypeStruct((t, D_MODEL), BF16),
                   jax.ShapeDtypeStruct((t, LANES), F32),
                   jax.ShapeDtypeStruct((SUBLANES, t), F32),
                   jax.ShapeDtypeStruct((nt * SUBLANES, LANES), F32),
                   jax.ShapeDtypeStruct((SUBLANES, LANES), F32)],
        scratch_shapes=[pltpu.VMEM((1, LANES), F32)],
        compiler_params=_cparams(("arbitrary",)),
        name="merge",
    )(o_hg, o_da, sgh, sgd, x2, wbh, wbd, wo, norm_moe, wr, br)


def _for_each_chunk(tab_ref, i, fn):
    def per_expert(e, carry):
        g0 = tab_ref[i, TAB_GSTART + e]
        l0 = tab_ref[i, TAB_LSTART + e]

        def per_chunk(c, carry2):
            off = c * MOE_CHUNK
            fn(pl.multiple_of(l0 + off, MOE_CHUNK), pl.multiple_of(g0 + off, MOE_CHUNK))
            return carry2

        lax.fori_loop(0, tab_ref[i, TAB_NCHUNK + e], per_chunk, 0)
        return carry

    lax.fori_loop(0, N_EXPERTS, per_expert, 0)


def _dispatch_kernel(tab_ref, tail_ref, metat_ref, xn_ref, xg_hbm, xp_ref, zero_ref, sem, sem_z):
    i = pl.program_id(0)
    lo1 = metat_ref[2:3, :]
    lo2 = metat_ref[3:4, :]
    r = lax.broadcasted_iota(jnp.int32, (MOE_LOCAL, xn_ref.shape[0]), 0).astype(F32)
    perm = jnp.where((r == lo1) | (r == lo2), 1.0, 0.0).astype(BF16)
    xp_ref[...] = jnp.dot(perm, xn_ref[...], preferred_element_type=F32).astype(BF16)

    def chunk_copy(lrow, grow):
        return pltpu.make_async_copy(xp_ref.at[pl.ds(lrow, MOE_CHUNK), :], xg_hbm.at[pl.ds(grow, MOE_CHUNK), :], sem)

    _for_each_chunk(tab_ref, i, lambda lrow, grow: chunk_copy(lrow, grow).start())

    def tail_copy(e, c):
        grow = pl.multiple_of(tail_ref[e] + c * MOE_CHUNK, MOE_CHUNK)
        return pltpu.make_async_copy(zero_ref.at[pl.ds(0, MOE_CHUNK), :], xg_hbm.at[pl.ds(grow, MOE_CHUNK), :], sem_z)

    def for_each_tail(fn):
        def per_expert(e, carry):
            lax.fori_loop(0, tail_ref[N_EXPERTS + e], lambda c, c2: (fn(e, c), c2)[1], 0)
            return carry
        lax.fori_loop(0, N_EXPERTS, per_expert, 0)

    def block_copy(b):
        grow = pl.multiple_of(b * TM_EXPERT, TM_EXPERT)
        return pltpu.make_async_copy(zero_ref, xg_hbm.at[pl.ds(grow, TM_EXPERT), :], sem_z)

    def for_each_free_block(fn):
        lax.fori_loop(tail_ref[2 * N_EXPERTS], xg_hbm.shape[0] // TM_EXPERT, lambda b, c2: (fn(b), c2)[1], 0)

    @pl.when(i == pl.num_programs(0) - 1)
    def _():
        zero_ref[...] = jnp.zeros_like(zero_ref)
        for_each_tail(lambda e, c: tail_copy(e, c).start())
        for_each_free_block(lambda b: block_copy(b).start())
        for_each_tail(lambda e, c: tail_copy(e, c).wait())
        for_each_free_block(lambda b: block_copy(b).wait())

    _for_each_chunk(tab_ref, i, lambda lrow, grow: chunk_copy(lrow, grow).wait())


def _dispatch(tab, tail, metat, xn, n_rows):
    t = xn.shape[0]
    tm = TM_MOE
    grid_spec = pltpu.PrefetchScalarGridSpec(
        num_scalar_prefetch=2,
        grid=(t // tm,),
        in_specs=[pl.BlockSpec((SUBLANES, tm), lambda i, tab, tail: (0, i)),
                  pl.BlockSpec((tm, D_MODEL), lambda i, tab, tail: (i, 0))],
        out_specs=pl.BlockSpec(memory_space=pl.ANY),
        scratch_shapes=[pltpu.VMEM((MOE_LOCAL, D_MODEL), BF16), pltpu.VMEM((TM_EXPERT, D_MODEL), BF16),
                        pltpu.SemaphoreType.DMA, pltpu.SemaphoreType.DMA],
    )
    return pl.pallas_call(
        _dispatch_kernel,
        grid_spec=grid_spec,
        out_shape=jax.ShapeDtypeStruct((n_rows, D_MODEL), BF16),
        compiler_params=_cparams(("arbitrary",)),
        name="dispatch",
    )(tab, tail, metat, xn)


def _expert_kernel(be_ref, nact_ref, xg_ref, w1_ref, w3_ref, w2_ref, y_ref, w1b_ref, w3b_ref, w2b_ref):
    i = pl.program_id(0)
    active = i < nact_ref[0]

    @pl.when(active & ((i == 0) | (be_ref[i] != be_ref[jnp.maximum(i - 1, 0)])))
    def _():
        w1b_ref[...] = w1_ref[0].astype(BF16)
        w3b_ref[...] = w3_ref[0].astype(BF16)
        w2b_ref[...] = w2_ref[0].astype(BF16)

    @pl.when(active)
    def _():
        xb = xg_ref[...]
        a = jnp.dot(xb, w1b_ref[...], preferred_element_type=F32)
        b = jnp.dot(xb, w3b_ref[...], preferred_element_type=F32)
        hid = (a * _sigmoid(a) * b).astype(BF16)
        y_ref[...] = jnp.dot(hid, w2b_ref[...], preferred_element_type=F32).astype(BF16)

    @pl.when(jnp.logical_not(active))
    def _():
        y_ref[...] = jnp.zeros_like(y_ref)


def _experts(block_expert, n_active, xg, w1, w3, w2):
    tm = TM_EXPERT
    nb = xg.shape[0] // tm
    blk = lambda i, be, na: jnp.minimum(i, na[0] - 1)
    grid_spec = pltpu.PrefetchScalarGridSpec(
        num_scalar_prefetch=2,
        grid=(nb,),
        in_specs=[pl.BlockSpec((tm, D_MODEL), lambda i, be, na: (blk(i, be, na), 0)),
                  pl.BlockSpec((1, D_MODEL, D_FF), lambda i, be, na: (be[blk(i, be, na)], 0, 0)),
                  pl.BlockSpec((1, D_MODEL, D_FF), lambda i, be, na: (be[blk(i, be, na)], 0, 0)),
                  pl.BlockSpec((1, D_FF, D_MODEL), lambda i, be, na: (be[blk(i, be, na)], 0, 0))],
        out_specs=pl.BlockSpec((tm, D_MODEL), lambda i, be, na: (i, 0)),
        scratch_shapes=[pltpu.VMEM((D_MODEL, D_FF), BF16), pltpu.VMEM((D_MODEL, D_FF), BF16),
                        pltpu.VMEM((D_FF, D_MODEL), BF16)],
    )
    return pl.pallas_call(
        _expert_kernel,
        grid_spec=grid_spec,
        out_shape=jax.ShapeDtypeStruct(xg.shape, BF16),
        compiler_params=_cparams(("arbitrary",)),
        name="experts",
    )(block_expert, n_active, xg, w1, w3, w2)


def _combine_kernel(tab_ref, meta_ref, xmid_ref, y_hbm, o_ref, yl_ref, sem):
    tm = xmid_ref.shape[0]
    i = pl.program_id(0)

    @pl.when(i == 0)
    def _():
        yl_ref[...] = jnp.zeros_like(yl_ref)

    def chunk_copy(lrow, grow):
        return pltpu.make_async_copy(y_hbm.at[pl.ds(grow, MOE_CHUNK), :], yl_ref.at[pl.ds(lrow, MOE_CHUNK), :], sem)

    _for_each_chunk(tab_ref, i, lambda lrow, grow: chunk_copy(lrow, grow).start())

    meta = meta_ref[...]
    lane = lax.broadcasted_iota(jnp.int32, (tm, MOE_LOCAL), 1).astype(F32)
    wsel = (jnp.where(lane == meta[:, 2:3], meta[:, 4:5], 0.0)
            + jnp.where(lane == meta[:, 3:4], meta[:, 5:6], 0.0)).astype(BF16)

    _for_each_chunk(tab_ref, i, lambda lrow, grow: chunk_copy(lrow, grow).wait())
    o_ref[...] = xmid_ref[...] + jnp.dot(wsel, yl_ref[...], preferred_element_type=F32)


def _combine(tab, meta, xmid, y):
    t = xmid.shape[0]
    tm = TM_MOE
    grid_spec = pltpu.PrefetchScalarGridSpec(
        num_scalar_prefetch=1,
        grid=(t // tm,),
        in_specs=[pl.BlockSpec((tm, LANES), lambda i, tab: (i, 0)),
                  pl.BlockSpec((tm, D_MODEL), lambda i, tab: (i, 0)),
                  pl.BlockSpec(memory_space=pl.ANY)],
        out_specs=pl.BlockSpec((tm, D_MODEL), lambda i, tab: (i, 0)),
        scratch_shapes=[pltpu.VMEM((MOE_LOCAL, D_MODEL), BF16), pltpu.SemaphoreType.DMA],
    )
    return pl.pallas_call(
        _combine_kernel,
        grid_spec=grid_spec,
        out_shape=jax.ShapeDtypeStruct((t, D_MODEL), F32),
        compiler_params=_cparams(("arbitrary",)),
        name="combine",
    )(tab, meta, xmid, y)


def kernel(x, norm_mix, w_in, hg_lb, hg_out_norm, da_q_norm, da_k_norm, da_lambda, da_out_norm, w_branch_hg,
           w_branch_da, w_out, norm_moe, w_router_group, b_router_group, w_router_expert, b_router_expert,
           w1, w3, w2):
    batch, seq, d = x.shape
    assert d == D_MODEL and norm_mix.shape[0] == 1 and w_in.shape[2] == IN_COLS
    assert seq % TC_HGRN == 0 and seq % TQ_ATTN == 0 and (batch * seq) % TM_MOE == 0
    t = batch * seq
    x2 = x.reshape(t, d)

    half = DA_HEAD // 2
    inv = ROPE_THETA ** (-jnp.arange(half, dtype=F32) / half)
    ang = jnp.arange(seq, dtype=F32)[:, None] * inv[None, :]
    cos_t = jnp.tile(jnp.cos(ang), (1, 2 * LANES // DA_HEAD))
    sin_t = jnp.tile(jnp.concatenate([-jnp.sin(ang), jnp.sin(ang)], axis=1), (1, LANES // DA_HEAD))

    reps = DA_WIDTH // DA_HEAD
    gq = jnp.tile(da_q_norm[0].astype(F32) * (DA_HEAD ** -0.5 * math.log2(math.e)), reps)[None, :]
    gk = jnp.tile(da_k_norm[0].astype(F32), reps)[None, :]

    hq, g, kk, hi, og, qn, kn, dv, sgh, sgd = _in_proj(
        x2, norm_mix.astype(F32), w_in[0].astype(BF16), hg_lb.astype(F32), gq, gk, cos_t, sin_t, seq)

    o_hg = _hgrn(hq, g, kk, hi, og, hg_out_norm.astype(F32), batch, seq)
    o_da = _attn(da_lambda[0].astype(F32), qn, kn, dv, da_out_norm.astype(F32), batch, seq)

    wr = jnp.zeros((D_MODEL, LANES), F32)
    wr = wr.at[:, :N_GROUPS].set(w_router_group[0]).at[:, N_GROUPS:N_GROUPS + N_EXPERTS].set(w_router_expert[0])
    br = jnp.zeros((1, LANES), F32)
    br = br.at[0, :N_GROUPS].set(b_router_group[0]).at[0, N_GROUPS:N_GROUPS + N_EXPERTS].set(b_router_expert[0])

    xmid, xn, meta, metat, ttab, cnt = _merge(
        o_hg, o_da, sgh, sgd, x2, w_branch_hg[0].astype(BF16), w_branch_da[0].astype(BF16), w_out[0].astype(BF16),
        norm_moe.astype(F32), wr, br)

    tmx = TM_EXPERT
    nt = t // TM_MOE
    ttab = ttab.reshape(nt, SUBLANES, LANES)[:, :, :N_EXPERTS].astype(jnp.int32)
    counts = cnt[0, :N_EXPERTS].astype(jnp.int32)
    padded = (counts + tmx - 1) // tmx * tmx
    pad_end = jnp.cumsum(padded)
    pad_start = pad_end - padded
    tab = jnp.concatenate([pad_start[None, :] + ttab[:, 0], ttab[:, 2], ttab[:, 1] // MOE_CHUNK,
                           jnp.zeros((nt, LANES - 3 * N_EXPERTS), jnp.int32)], axis=1)
    tail = jnp.concatenate([pad_start + counts, (padded - counts) // MOE_CHUNK, pad_end[-1:] // tmx])
    n_rows = t * TOP_K + nt * N_EXPERTS * MOE_CHUNK + N_EXPERTS * tmx
    nb = n_rows // tmx
    block_start = jnp.arange(nb, dtype=jnp.int32) * tmx
    block_expert = jnp.minimum(jnp.sum(pad_end[None, :] <= block_start[:, None], axis=1),
                               N_EXPERTS - 1).astype(jnp.int32)
    n_active = (pad_end[-1:] // tmx).astype(jnp.int32)

    xg = _dispatch(tab, tail, metat, xn, n_rows)
    y = _experts(block_expert, n_active, xg, w1[0], w3[0], w2[0])
    out = _combine(tab, meta, xmid, y)
    return out.reshape(batch, seq, d)
```

```python
import functools
import math

import jax
import jax.numpy as jnp
from jax import lax
from jax.experimental import pallas as pl
from jax.experimental.pallas import tpu as pltpu

F32 = jnp.float32
BF16 = jnp.bfloat16

D_MODEL = 1024
HG_HEADS = 4
HG_DK = 128
HG_WIDTH = HG_HEADS * HG_DK
HG_CHUNK = 64
DA_HEADS = 4
DA_HEAD = 64
DA_VDIM = 2 * DA_HEAD
DA_WIDTH = DA_HEADS * DA_VDIM
ROPE_THETA = 10000.0
N_GROUPS = 4
EXPERTS_PER_GROUP = 8
N_EXPERTS = N_GROUPS * EXPERTS_PER_GROUP
TOP_K = 2
D_FF = 512
EPS = 1e-6
LAM_INIT = 0.8 - 0.6 * math.exp(-0.3 * 0)
IN_COLS = 4 * HG_WIDTH + 3 * DA_WIDTH + 2 * D_MODEL

LANES = 128
SUBLANES = 8
BF16_ROWS = 16
NEG = -1e30

TM_PROJ = 512
TC_HGRN = 512
HGRN_GROUP = 256
TQ_ATTN = 512
TM_MOE = 512
TM_EXPERT = 256
MOE_CHUNK = BF16_ROWS
MOE_LOCAL = TOP_K * TM_MOE + N_EXPERTS * MOE_CHUNK
TAB_GSTART, TAB_LSTART, TAB_NCHUNK = 0, N_EXPERTS, 2 * N_EXPERTS
VMEM_LIMIT = 56 * 1024 * 1024


def _cparams(sem):
    return pltpu.CompilerParams(dimension_semantics=sem, vmem_limit_bytes=VMEM_LIMIT)


def _sigmoid(v):
    return 1.0 / (1.0 + jnp.exp(-v))


def _qk_prep(d, gain, cos, sin, gmat):
    ss = jnp.dot((d * d).astype(BF16), gmat, preferred_element_type=F32)
    y = d * lax.rsqrt(ss * (1.0 / DA_HEAD) + EPS) * gain
    lane = lax.broadcasted_iota(jnp.int32, (d.shape[0], LANES), 1)
    upper = (lane & (DA_HEAD // 2)) != 0
    outs = []
    for c in range(d.shape[1] // LANES):
        yc = y[:, c * LANES:(c + 1) * LANES]
        sw = jnp.where(upper, pltpu.roll(yc, DA_HEAD // 2, 1), pltpu.roll(yc, LANES - DA_HEAD // 2, 1))
        outs.append(yc * cos + sw * sin)
    return jnp.concatenate(outs, axis=1)


def _in_proj_kernel(x_ref, nm_ref, w_ref, lb_ref, gq_ref, gk_ref, cos_ref, sin_ref,
                    hq_ref, g_ref, kk_ref, hi_ref, og_ref, qn_ref, kn_ref, dv_ref, sgh_ref, sgd_ref):
    x = x_ref[...]
    h = x * lax.rsqrt(jnp.mean(x * x, axis=-1, keepdims=True) + EPS) * nm_ref[...]
    hb = h.astype(BF16)

    def proj(c0, width):
        return jnp.dot(hb, w_ref[:, c0:c0 + width], preferred_element_type=F32)

    hq_ref[...] = proj(0, HG_WIDTH).astype(BF16)

    hf = proj(HG_WIDTH, HG_WIDTH)
    lbp = lb_ref[...]
    mx = jnp.maximum(lbp[0:1], lbp[1:2])
    e0 = jnp.exp(lbp[0:1] - mx)
    e1 = jnp.exp(lbp[1:2] - mx)
    lb = e0 / (e0 + e1)
    f = lb + (1.0 - lb) * _sigmoid(hf)
    g_ref[...] = jnp.log(f)
    kk_ref[...] = (1.0 - f).astype(BF16)

    hi_ref[...] = proj(2 * HG_WIDTH, HG_WIDTH).astype(BF16)
    hog = proj(3 * HG_WIDTH, HG_WIDTH)
    og_ref[...] = (hog * _sigmoid(hog)).astype(BF16)

    r = lax.broadcasted_iota(jnp.int32, (DA_WIDTH, DA_WIDTH), 0) // DA_HEAD
    c = lax.broadcasted_iota(jnp.int32, (DA_WIDTH, DA_WIDTH), 1) // DA_HEAD
    gmat = jnp.where(r == c, 1.0, 0.0).astype(BF16)
    cos = cos_ref[...]
    sin = sin_ref[...]
    base = 4 * HG_WIDTH
    qn_ref[...] = _qk_prep(proj(base, DA_WIDTH), gq_ref[...], cos, sin, gmat).astype(BF16)
    kn_ref[...] = _qk_prep(proj(base + DA_WIDTH, DA_WIDTH), gk_ref[...], cos, sin, gmat).astype(BF16)
    dv_ref[...] = proj(base + 2 * DA_WIDTH, DA_WIDTH).astype(BF16)
    base += 3 * DA_WIDTH
    sgh_ref[...] = _sigmoid(proj(base, D_MODEL)).astype(BF16)
    sgd_ref[...] = _sigmoid(proj(base + D_MODEL, D_MODEL)).astype(BF16)


def _in_proj(x2, norm_mix, w_in_bf, hg_lb, gq, gk, cos_t, sin_t, seq):
    t = x2.shape[0]
    tm = TM_PROJ
    nseq = seq // tm
    row = lambda w: pl.BlockSpec((tm, w), lambda i: (i, 0))
    const = lambda shape: pl.BlockSpec(shape, lambda i: (0, 0))
    tab = pl.BlockSpec((tm, LANES), lambda i: (i % nseq, 0))
    out_shape = [
        jax.ShapeDtypeStruct((t, HG_WIDTH), BF16),
        jax.ShapeDtypeStruct((t, HG_WIDTH), F32),
        jax.ShapeDtypeStruct((t, HG_WIDTH), BF16),
        jax.ShapeDtypeStruct((t, HG_WIDTH), BF16),
        jax.ShapeDtypeStruct((t, HG_WIDTH), BF16),
        jax.ShapeDtypeStruct((t, DA_WIDTH), BF16),
        jax.ShapeDtypeStruct((t, DA_WIDTH), BF16),
        jax.ShapeDtypeStruct((t, DA_WIDTH), BF16),
        jax.ShapeDtypeStruct((t, D_MODEL), BF16),
        jax.ShapeDtypeStruct((t, D_MODEL), BF16),
    ]
    out_specs = [row(HG_WIDTH)] * 5 + [row(DA_WIDTH)] * 3 + [row(D_MODEL)] * 2
    return pl.pallas_call(
        _in_proj_kernel,
        grid=(t // tm,),
        in_specs=[row(D_MODEL), const((1, D_MODEL)), const((D_MODEL, IN_COLS)), const((2, HG_WIDTH)),
                  const((1, DA_WIDTH)), const((1, DA_WIDTH)), tab, tab],
        out_specs=out_specs,
        out_shape=out_shape,
        compiler_params=_cparams(("parallel",)),
        name="in_proj",
    )(x2, norm_mix, w_in_bf, hg_lb, gq, gk, cos_t, sin_t)


def _hgrn_kernel(q_ref, g_ref, k_ref, v_ref, og_ref, gain_ref, o_ref, st_ref):
    @pl.when(pl.program_id(1) == 0)
    def _():
        st_ref[...] = jnp.zeros_like(st_ref)

    c = HG_CHUNK
    tc = q_ref.shape[0]
    nc = tc // c
    grp = HGRN_GROUP
    row = lax.broadcasted_iota(jnp.int32, (grp, grp), 0)
    col = lax.broadcasted_iota(jnp.int32, (grp, grp), 1)
    causal = (row // c == col // c) & (col <= row)
    tri = jnp.where(causal, 1.0, 0.0).astype(BF16)
    gain = gain_ref[...]
    nt = (((1,), (1,)), ((), ()))
    pair_lane = lax.broadcasted_iota(jnp.int32, (HG_DK, 2 * c), 1)
    in_chunk = [pair_lane < c, pair_lane >= c]

    g = g_ref[...]
    g_hi = g.astype(BF16)
    g_lo = (g - g_hi.astype(F32)).astype(BF16)
    b = jnp.concatenate(
        [jnp.dot(tri, g_hi[r0:r0 + grp], preferred_element_type=F32)
         + jnp.dot(tri, g_lo[r0:r0 + grp], preferred_element_type=F32) for r0 in range(0, tc, grp)], axis=0)
    b_mid_rows = [b[ci * c + c // 2 - 1:ci * c + c // 2, :] for ci in range(nc)]
    b_end_rows = [b[ci * c + c - 1:ci * c + c, :] for ci in range(nc)]
    per_chunk = lambda rows: jnp.concatenate([jnp.broadcast_to(r, (c, r.shape[1])) for r in rows], axis=0)
    b_mid = per_chunk(b_mid_rows)
    e_fwd = jnp.exp(b - b_mid)
    e_bwd = jnp.exp(b_mid - b)
    q_rel32 = q_ref[...].astype(F32) * e_fwd
    k_rel32 = k_ref[...].astype(F32) * e_bwd
    q_rel = q_rel32.astype(BF16)
    k_rel = k_rel32.astype(BF16)
    q_st = (q_rel32 * per_chunk([jnp.exp(r) for r in b_mid_rows])).astype(BF16)
    k_end = (k_rel32 * per_chunk([jnp.exp(e - m) for e, m in zip(b_end_rows, b_mid_rows)])).astype(BF16)
    decay = [jnp.exp(r) for r in b_end_rows]
    v = v_ref[...]
    og = og_ref[...]

    for h in range(HG_HEADS):
        cols = slice(h * HG_DK, (h + 1) * HG_DK)
        intra = []
        for r0 in range(0, tc, grp):
            rows = slice(r0, r0 + grp)
            sc = lax.dot_general(q_rel[rows, cols], k_rel[rows, cols], nt, preferred_element_type=F32)
            sc = jnp.where(causal, sc, 0.0).astype(BF16)
            intra.append(jnp.dot(sc, v[rows, cols], preferred_element_type=F32))
        v_t = v[:, cols].astype(F32).T.astype(BF16)
        incr = []
        for ci in range(nc):
            pair = slice((ci // 2) * 2 * c, (ci // 2 + 1) * 2 * c)
            lhs = jnp.where(in_chunk[ci % 2], v_t[:, pair], jnp.zeros((HG_DK, 2 * c), BF16))
            incr.append(jnp.dot(lhs, k_end[pair, cols], preferred_element_type=F32))
        st = st_ref[h]
        inter = []
        for ci in range(nc):
            rows = slice(ci * c, (ci + 1) * c)
            inter.append(lax.dot_general(q_st[rows, cols], st.astype(BF16), nt, preferred_element_type=F32))
            st = st * decay[ci][:, cols] + incr[ci]
        st_ref[h] = st
        o = jnp.concatenate(intra, axis=0) + jnp.concatenate(inter, axis=0)
        on = o * lax.rsqrt(jnp.mean(o * o, axis=-1, keepdims=True) + EPS) * gain
        o_ref[:, cols] = (on * og[:, cols].astype(F32)).astype(BF16)


def _hgrn(hq, g, kk, hi, og, gain, batch, seq):
    t = hq.shape[0]
    tc = TC_HGRN
    ns = seq // tc
    spec = pl.BlockSpec((tc, HG_WIDTH), lambda b, s: (b * ns + s, 0))
    return pl.pallas_call(
        _hgrn_kernel,
        grid=(batch, ns),
        in_specs=[spec] * 5 + [pl.BlockSpec((1, HG_DK), lambda b, s: (0, 0))],
        out_specs=spec,
        out_shape=jax.ShapeDtypeStruct((t, HG_WIDTH), BF16),
        scratch_shapes=[pltpu.VMEM((HG_HEADS, HG_DK, HG_DK), F32)],
        compiler_params=_cparams(("parallel", "arbitrary")),
        name="hgrn",
    )(hq, g, kk, hi, og, gain)


def _attn_kernel(lam_ref, q_ref, k_ref, v_ref, gain_ref, o_ref, s_ref, sd_ref, m_ref, acc_ref):
    tq = TQ_ATTN
    dv = DA_VDIM
    nq = q_ref.shape[0] // tq
    nt = (((1,), (1,)), ((), ()))
    ones = jnp.ones((tq, dv), BF16)
    lane = lax.broadcasted_iota(jnp.int32, (tq, dv), 1)
    zero = jnp.zeros((tq, dv), BF16)
    r = lax.broadcasted_iota(jnp.int32, (2 * tq, tq), 0)
    c = lax.broadcasted_iota(jnp.int32, (2 * tq, tq), 1)
    on_or_below_diag = c <= jnp.where(r >= tq, r - tq, r)
    lp = lam_ref[...]
    lam = (jnp.exp(jnp.sum(lp[0:1] * lp[1:2], axis=1, keepdims=True))
           - jnp.exp(jnp.sum(lp[2:3] * lp[3:4], axis=1, keepdims=True)) + LAM_INIT)
    out_gain = gain_ref[...] * (1.0 - LAM_INIT)

    def block_rows(j):
        return pl.ds(pl.multiple_of(j * tq, tq), tq)

    def stacked_q(i):
        q = q_ref[block_rows(i), :]
        return jnp.concatenate([jnp.where(lane < DA_HEAD, q, zero), jnp.where(lane >= DA_HEAD, q, zero)], axis=0)

    def scores(qq, j):
        return lax.dot_general(qq, k_ref[block_rows(j), :], nt, preferred_element_type=F32)

    def diag_scores(qq, i):
        return jnp.where(on_or_below_diag, scores(qq, i), NEG)

    def accumulate(s, j):
        v1 = jnp.concatenate([v_ref[block_rows(j), :], ones], axis=1)
        m_prev = m_ref[...]
        m_new = jnp.maximum(m_prev, jnp.max(s, axis=1, keepdims=True))
        alpha = jnp.exp2(m_prev - m_new)
        p = jnp.exp2(s - jnp.tile(m_new, (1, tq // LANES)))
        acc_ref[...] = (jnp.tile(alpha, (1, 2)) * acc_ref[...]
                        + jnp.dot(p.astype(BF16), v1, preferred_element_type=F32))
        m_ref[...] = m_new

    def reset():
        m_ref[...] = jnp.full_like(m_ref, NEG)
        acc_ref[...] = jnp.zeros_like(acc_ref)

    def finish(i):
        accumulate(sd_ref[...], i)
        acc = acc_ref[...]
        o = acc[:tq, :dv] / acc[:tq, dv:] - lam * (acc[tq:, :dv] / acc[tq:, dv:])
        on = o * lax.rsqrt(jnp.mean(o * o, axis=-1, keepdims=True) + EPS) * out_gain
        o_ref[block_rows(i), :] = on.astype(BF16)

    reset()
    sd_ref[...] = diag_scores(stacked_q(0), 0)

    def q_block(i, carry):
        qq = stacked_q(i)
        s_ref[0] = scores(qq, 0)
        finish(i - 1)
        reset()

        def pair(m, carry2):
            s_ref[1] = scores(qq, 2 * m + 1)
            accumulate(s_ref[0], 2 * m)
            s_ref[0] = scores(qq, 2 * m + 2)
            accumulate(s_ref[1], 2 * m + 1)
            return carry2

        lax.fori_loop(0, (i - 1) // 2, pair, 0)

        @pl.when(i % 2 == 0)
        def _():
            s_ref[1] = scores(qq, i - 1)
            accumulate(s_ref[0], i - 2)

        sd_ref[...] = diag_scores(qq, i)
        accumulate(s_ref[(i - 1) % 2], i - 1)
        return carry

    lax.fori_loop(1, nq, q_block, 0)
    finish(nq - 1)


def _attn(lam_p, qn, kn, dv, gain, batch, seq):
    t = qn.shape[0]
    tq = TQ_ATTN
    head = lambda b, h: (b, h)
    return pl.pallas_call(
        _attn_kernel,
        grid=(batch, DA_HEADS),
        in_specs=[pl.BlockSpec((4, DA_HEAD), lambda b, h: (0, 0)),
                  pl.BlockSpec((seq, DA_VDIM), head),
                  pl.BlockSpec((seq, DA_VDIM), head),
                  pl.BlockSpec((seq, DA_VDIM), head),
                  pl.BlockSpec((1, DA_VDIM), lambda b, h: (0, 0))],
        out_specs=pl.BlockSpec((seq, DA_VDIM), head),
        out_shape=jax.ShapeDtypeStruct((t, DA_WIDTH), BF16),
        scratch_shapes=[pltpu.VMEM((2, 2 * tq, tq), F32), pltpu.VMEM((2 * tq, tq), F32),
                        pltpu.VMEM((2 * tq, LANES), F32), pltpu.VMEM((2 * tq, 2 * DA_VDIM), F32)],
        compiler_params=_cparams(("parallel", "parallel")),
        name="attn",
    )(lam_p, qn, kn, dv, gain)


def _merge_kernel(ohg_ref, oda_ref, sgh_ref, sgd_ref, x_ref, wbh_ref, wbd_ref, wo_ref, nm_ref, wr_ref, br_ref,
                  xmid_ref, xn_ref, meta_ref, metat_ref, ttab_ref, cnt_ref, carry_ref):
    tm = x_ref.shape[0]

    @pl.when(pl.program_id(0) == 0)
    def _():
        carry_ref[...] = jnp.zeros_like(carry_ref)

    y_hg = jnp.dot(ohg_ref[...], wbh_ref[...], preferred_element_type=F32)
    y_da = jnp.dot(oda_ref[...], wbd_ref[...], preferred_element_type=F32)
    mixed = sgh_ref[...].astype(F32) * y_hg + sgd_ref[...].astype(F32) * y_da
    xm = x_ref[...] + jnp.dot(mixed.astype(BF16), wo_ref[...], preferred_element_type=F32)
    xmid_ref[...] = xm
    xn = xm * lax.rsqrt(jnp.mean(xm * xm, axis=-1, keepdims=True) + EPS) * nm_ref[...]
    xn_ref[...] = xn.astype(BF16)

    logit = jnp.dot(xn.astype(BF16), wr_ref[...].astype(BF16), preferred_element_type=F32) + br_ref[...]

    lane = lax.broadcasted_iota(jnp.int32, (tm, LANES), 1)
    lanef = lane.astype(F32)
    big = float(LANES)
    is_g = lane < N_GROUPS
    gl = jnp.where(is_g, logit, NEG)
    gmax = jnp.max(gl, axis=1, keepdims=True)
    g_sel = jnp.min(jnp.where(gl == gmax, lanef, big), axis=1, keepdims=True)
    g_w = 1.0 / jnp.sum(jnp.where(is_g, jnp.exp(gl - gmax), 0.0), axis=1, keepdims=True)
    lo = N_GROUPS + EXPERTS_PER_GROUP * g_sel
    in_grp = (lanef >= lo) & (lanef < lo + EXPERTS_PER_GROUP)
    el = jnp.where(in_grp, logit, NEG)
    v1 = jnp.max(el, axis=1, keepdims=True)
    i1 = jnp.min(jnp.where(el == v1, lanef, big), axis=1, keepdims=True)
    el2 = jnp.where(lanef == i1, NEG, el)
    v2 = jnp.max(el2, axis=1, keepdims=True)
    i2 = jnp.min(jnp.where(el2 == v2, lanef, big), axis=1, keepdims=True)
    e1 = i1 - N_GROUPS
    e2 = i2 - N_GROUPS
    d = jnp.exp(v2 - v1)
    w1 = g_w / (1.0 + d)
    w2 = g_w * d / (1.0 + d)

    oh1 = jnp.where(lanef == e1, 1.0, 0.0)
    oh2 = jnp.where(lanef == e2, 1.0, 0.0)
    r = lax.broadcasted_iota(jnp.int32, (tm, tm), 0)
    c = lax.broadcasted_iota(jnp.int32, (tm, tm), 1)
    below = jnp.where(c < r, 1.0, 0.0).astype(BF16)
    c1 = jnp.dot(below, oh1.astype(BF16), preferred_element_type=F32)
    c2 = jnp.dot(below, oh2.astype(BF16), preferred_element_type=F32)
    tot1 = jnp.sum(oh1, axis=0, keepdims=True)
    tot2 = jnp.sum(oh2, axis=0, keepdims=True)
    n_chunk = jnp.floor((tot1 + tot2 + (MOE_CHUNK - 1)) * (1.0 / MOE_CHUNK))
    r = lax.broadcasted_iota(jnp.int32, (LANES, LANES), 0)
    c = lax.broadcasted_iota(jnp.int32, (LANES, LANES), 1)
    before = jnp.where(r < c, 1.0, 0.0).astype(BF16)
    lstart = MOE_CHUNK * jnp.dot(jnp.broadcast_to(n_chunk, (SUBLANES, LANES)).astype(BF16), before,
                                 preferred_element_type=F32)[0:1]
    lo1 = jnp.sum(oh1 * (c1 + lstart), axis=1, keepdims=True)
    lo2 = jnp.sum(oh2 * (c2 + lstart + tot1), axis=1, keepdims=True)
    carry = carry_ref[...]
    ttab_ref[...] = jnp.concatenate([carry, MOE_CHUNK * n_chunk, lstart, jnp.zeros((SUBLANES - 3, LANES), F32)], axis=0)
    carry = carry + MOE_CHUNK * n_chunk
    carry_ref[...] = carry
    cnt_ref[...] = jnp.broadcast_to(carry, cnt_ref.shape)

    meta = jnp.where(lane == 0, e1, jnp.where(lane == 1, e2, jnp.where(lane == 2, lo1, jnp.where(
        lane == 3, lo2, jnp.where(lane == 4, w1, jnp.where(lane == 5, w2, 0.0))))))
    meta_ref[...] = meta
    metat_ref[...] = meta.T[0:SUBLANES, :]


def _merge(o_hg, o_da, sgh, sgd, x2, wbh, wbd, wo, norm_moe, wr, br):
    t = x2.shape[0]
    tm = TM_MOE
    nt = t // tm
    row = lambda w: pl.BlockSpec((tm, w), lambda i: (i, 0))
    const = lambda shape: pl.BlockSpec(shape, lambda i: (0, 0))
    return pl.pallas_call(
        _merge_kernel,
        grid=(nt,),
        in_specs=[row(HG_WIDTH), row(DA_WIDTH), row(D_MODEL), row(D_MODEL), row(D_MODEL),
                  const((HG_WIDTH, D_MODEL)), const((DA_WIDTH, D_MODEL)), const((D_MODEL, D_MODEL)),
                  const((1, D_MODEL)), const((D_MODEL, LANES)), const((1, LANES))],
        out_specs=[row(D_MODEL), row(D_MODEL), row(LANES),
                   pl.BlockSpec((SUBLANES, tm), lambda i: (0, i)),
                   pl.BlockSpec((SUBLANES, LANES), lambda i: (i, 0)),
                   const((SUBLANES, LANES))],
        out_shape=[jax.ShapeDtypeStruct((t, D_MODEL), F32),
                   jax.ShapeDtypeStruct((t, D_MODEL), BF16),
                   jax.ShapeDtypeStruct((t, LANES), F32),
                   jax.ShapeDtypeStruct((SUBLANES, t), F32),
                   jax.ShapeDtypeStruct((nt * SUBLANES, LANES), F32),
                   jax.ShapeDtypeStruct((SUBLANES, LANES), F32)],
        scratch_shapes=[pltpu.VMEM((1, LANES), F32)],
        compiler_params=_cparams(("arbitrary",)),
        name="merge",
    )(o_hg, o_da, sgh, sgd, x2, wbh, wbd, wo, norm_moe, wr, br)


def _for_each_chunk(tab_ref, i, fn):
    def per_expert(e, carry):
        g0 = tab_ref[i, TAB_GSTART + e]
        l0 = tab_ref[i, TAB_LSTART + e]

        def per_chunk(c, carry2):
            off = c * MOE_CHUNK
            fn(pl.multiple_of(l0 + off, MOE_CHUNK), pl.multiple_of(g0 + off, MOE_CHUNK))
            return carry2

        lax.fori_loop(0, tab_ref[i, TAB_NCHUNK + e], per_chunk, 0)
        return carry

    lax.fori_loop(0, N_EXPERTS, per_expert, 0)


def _dispatch_kernel(tab_ref, tail_ref, metat_ref, xn_ref, xg_hbm, xp_ref, zero_ref, sem, sem_z):
    i = pl.program_id(0)
    lo1 = metat_ref[2:3, :]
    lo2 = metat_ref[3:4, :]
    r = lax.broadcasted_iota(jnp.int32, (MOE_LOCAL, xn_ref.shape[0]), 0).astype(F32)
    perm = jnp.where((r == lo1) | (r == lo2), 1.0, 0.0).astype(BF16)
    xp_ref[...] = jnp.dot(perm, xn_ref[...], preferred_element_type=F32).astype(BF16)

    def chunk_copy(lrow, grow):
        return pltpu.make_async_copy(xp_ref.at[pl.ds(lrow, MOE_CHUNK), :], xg_hbm.at[pl.ds(grow, MOE_CHUNK), :], sem)

    _for_each_chunk(tab_ref, i, lambda lrow, grow: chunk_copy(lrow, grow).start())

    def tail_copy(e, c):
        grow = pl.multiple_of(tail_ref[e] + c * MOE_CHUNK, MOE_CHUNK)
        return pltpu.make_async_copy(zero_ref.at[pl.ds(0, MOE_CHUNK), :], xg_hbm.at[pl.ds(grow, MOE_CHUNK), :], sem_z)

    def for_each_tail(fn):
        def per_expert(e, carry):
            lax.fori_loop(0, tail_ref[N_EXPERTS + e], lambda c, c2: (fn(e, c), c2)[1], 0)
            return carry
        lax.fori_loop(0, N_EXPERTS, per_expert, 0)

    def block_copy(b):
        grow = pl.multiple_of(b * TM_EXPERT, TM_EXPERT)
        return pltpu.make_async_copy(zero_ref, xg_hbm.at[pl.ds(grow, TM_EXPERT), :], sem_z)

    def for_each_free_block(fn):
        lax.fori_loop(tail_ref[2 * N_EXPERTS], xg_hbm.shape[0] // TM_EXPERT, lambda b, c2: (fn(b), c2)[1], 0)

    @pl.when(i == pl.num_programs(0) - 1)
    def _():
        zero_ref[...] = jnp.zeros_like(zero_ref)
        for_each_tail(lambda e, c: tail_copy(e, c).start())
        for_each_free_block(lambda b: block_copy(b).start())
        for_each_tail(lambda e, c: tail_copy(e, c).wait())
        for_each_free_block(lambda b: block_copy(b).wait())

    _for_each_chunk(tab_ref, i, lambda lrow, grow: chunk_copy(lrow, grow).wait())


def _dispatch(tab, tail, metat, xn, n_rows):
    t = xn.shape[0]
    tm = TM_MOE
    grid_spec = pltpu.PrefetchScalarGridSpec(
        num_scalar_prefetch=2,
        grid=(t // tm,),
        in_specs=[pl.BlockSpec((SUBLANES, tm), lambda i, tab, tail: (0, i)),
                  pl.BlockSpec((tm, D_MODEL), lambda i, tab, tail: (i, 0))],
        out_specs=pl.BlockSpec(memory_space=pl.ANY),
        scratch_shapes=[pltpu.VMEM((MOE_LOCAL, D_MODEL), BF16), pltpu.VMEM((TM_EXPERT, D_MODEL), BF16),
                        pltpu.SemaphoreType.DMA, pltpu.SemaphoreType.DMA],
    )
    return pl.pallas_call(
        _dispatch_kernel,
        grid_spec=grid_spec,
        out_shape=jax.ShapeDtypeStruct((n_rows, D_MODEL), BF16),
        compiler_params=_cparams(("arbitrary",)),
        name="dispatch",
    )(tab, tail, metat, xn)


def _expert_kernel(be_ref, nact_ref, xg_ref, w1_ref, w3_ref, w2_ref, y_ref, w1b_ref, w3b_ref, w2b_ref):
    i = pl.program_id(0)
    active = i < nact_ref[0]

    @pl.when(active & ((i == 0) | (be_ref[i] != be_ref[jnp.maximum(i - 1, 0)])))
    def _():
        w1b_ref[...] = w1_ref[0].astype(BF16)
        w3b_ref[...] = w3_ref[0].astype(BF16)
        w2b_ref[...] = w2_ref[0].astype(BF16)

    @pl.when(active)
    def _():
        xb = xg_ref[...]
        a = jnp.dot(xb, w1b_ref[...], preferred_element_type=F32)
        b = jnp.dot(xb, w3b_ref[...], preferred_element_type=F32)
        hid = (a * _sigmoid(a) * b).astype(BF16)
        y_ref[...] = jnp.dot(hid, w2b_ref[...], preferred_element_type=F32).astype(BF16)

    @pl.when(jnp.logical_not(active))
    def _():
        y_ref[...] = jnp.zeros_like(y_ref)


def _experts(block_expert, n_active, xg, w1, w3, w2):
    tm = TM_EXPERT
    nb = xg.shape[0] // tm
    blk = lambda i, be, na: jnp.minimum(i, na[0] - 1)
    grid_spec = pltpu.PrefetchScalarGridSpec(
        num_scalar_prefetch=2,
        grid=(nb,),
        in_specs=[pl.BlockSpec((tm, D_MODEL), lambda i, be, na: (blk(i, be, na), 0)),
                  pl.BlockSpec((1, D_MODEL, D_FF), lambda i, be, na: (be[blk(i, be, na)], 0, 0)),
                  pl.BlockSpec((1, D_MODEL, D_FF), lambda i, be, na: (be[blk(i, be, na)], 0, 0)),
                  pl.BlockSpec((1, D_FF, D_MODEL), lambda i, be, na: (be[blk(i, be, na)], 0, 0))],
        out_specs=pl.BlockSpec((tm, D_MODEL), lambda i, be, na: (i, 0)),
        scratch_shapes=[pltpu.VMEM((D_MODEL, D_FF), BF16), pltpu.VMEM((D_MODEL, D_FF), BF16),
                        pltpu.VMEM((D_FF, D_MODEL), BF16)],
    )
    return pl.pallas_call(
        _expert_kernel,
        grid_spec=grid_spec,
        out_shape=jax.ShapeDtypeStruct(xg.shape, BF16),
        compiler_params=_cparams(("arbitrary",)),
        name="experts",
    )(block_expert, n_active, xg, w1, w3, w2)


def _combine_kernel(tab_ref, meta_ref, xmid_ref, y_hbm, o_ref, yl_ref, sem):
    tm = xmid_ref.shape[0]
    i = pl.program_id(0)

    @pl.when(i == 0)
    def _():
        yl_ref[...] = jnp.zeros_like(yl_ref)

    def chunk_copy(lrow, grow):
        return pltpu.make_async_copy(y_hbm.at[pl.ds(grow, MOE_CHUNK), :], yl_ref.at[pl.ds(lrow, MOE_CHUNK), :], sem)

    _for_each_chunk(tab_ref, i, lambda lrow, grow: chunk_copy(lrow, grow).start())

    meta = meta_ref[...]
    lane = lax.broadcasted_iota(jnp.int32, (tm, MOE_LOCAL), 1).astype(F32)
    wsel = (jnp.where(lane == meta[:, 2:3], meta[:, 4:5], 0.0)
            + jnp.where(lane == meta[:, 3:4], meta[:, 5:6], 0.0)).astype(BF16)

    _for_each_chunk(tab_ref, i, lambda lrow, grow: chunk_copy(lrow, grow).wait())
    o_ref[...] = xmid_ref[...] + jnp.dot(wsel, yl_ref[...], preferred_element_type=F32)


def _combine(tab, meta, xmid, y):
    t = xmid.shape[0]
    tm = TM_MOE
    grid_spec = pltpu.PrefetchScalarGridSpec(
        num_scalar_prefetch=1,
        grid=(t // tm,),
        in_specs=[pl.BlockSpec((tm, LANES), lambda i, tab: (i, 0)),
                  pl.BlockSpec((tm, D_MODEL), lambda i, tab: (i, 0)),
                  pl.BlockSpec(memory_space=pl.ANY)],
        out_specs=pl.BlockSpec((tm, D_MODEL), lambda i, tab: (i, 0)),
        scratch_shapes=[pltpu.VMEM((MOE_LOCAL, D_MODEL), BF16), pltpu.SemaphoreType.DMA],
    )
    return pl.pallas_call(
        _combine_kernel,
        grid_spec=grid_spec,
        out_shape=jax.ShapeDtypeStruct((t, D_MODEL), F32),
        compiler_params=_cparams(("arbitrary",)),
        name="combine",
    )(tab, meta, xmid, y)


def kernel(x, norm_mix, w_in, hg_lb, hg_out_norm, da_q_norm, da_k_norm, da_lambda, da_out_norm, w_branch_hg,
           w_branch_da, w_out, norm_moe, w_router_group, b_router_group, w_router_expert, b_router_expert,
           w1, w3, w2):
    batch, seq, d = x.shape
    assert d == D_MODEL and norm_mix.shape[0] == 1 and w_in.shape[2] == IN_COLS
    assert seq % TC_HGRN == 0 and seq % TQ_ATTN == 0 and (batch * seq) % TM_MOE == 0
    t = batch * seq
    x2 = x.reshape(t, d)

    half = DA_HEAD // 2
    inv = ROPE_THETA ** (-jnp.arange(half, dtype=F32) / half)
    ang = jnp.arange(seq, dtype=F32)[:, None] * inv[None, :]
    cos_t = jnp.tile(jnp.cos(ang), (1, 2 * LANES // DA_HEAD))
    sin_t = jnp.tile(jnp.concatenate([-jnp.sin(ang), jnp.sin(ang)], axis=1), (1, LANES // DA_HEAD))

    reps = DA_WIDTH // DA_HEAD
    gq = jnp.tile(da_q_norm[0].astype(F32) * (DA_HEAD ** -0.5 * math.log2(math.e)), reps)[None, :]
    gk = jnp.tile(da_k_norm[0].astype(F32), reps)[None, :]

    hq, g, kk, hi, og, qn, kn, dv, sgh, sgd = _in_proj(
        x2, norm_mix.astype(F32), w_in[0].astype(BF16), hg_lb.astype(F32), gq, gk, cos_t, sin_t, seq)

    o_hg = _hgrn(hq, g, kk, hi, og, hg_out_norm.astype(F32), batch, seq)
    o_da = _attn(da_lambda[0].astype(F32), qn, kn, dv, da_out_norm.astype(F32), batch, seq)

    wr = jnp.zeros((D_MODEL, LANES), F32)
    wr = wr.at[:, :N_GROUPS].set(w_router_group[0]).at[:, N_GROUPS:N_GROUPS + N_EXPERTS].set(w_router_expert[0])
    br = jnp.zeros((1, LANES), F32)
    br = br.at[0, :N_GROUPS].set(b_router_group[0]).at[0, N_GROUPS:N_GROUPS + N_EXPERTS].set(b_router_expert[0])

    xmid, xn, meta, metat, ttab, cnt = _merge(
        o_hg, o_da, sgh, sgd, x2, w_branch_hg[0].astype(BF16), w_branch_da[0].astype(BF16), w_out[0].astype(BF16),
        norm_moe.astype(F32), wr, br)

    tmx = TM_EXPERT
    nt = t // TM_MOE
    ttab = ttab.reshape(nt, SUBLANES, LANES)[:, :, :N_EXPERTS].astype(jnp.int32)
    counts = cnt[0, :N_EXPERTS].astype(jnp.int32)
    padded = (counts + tmx - 1) // tmx * tmx
    pad_end = jnp.cumsum(padded)
    pad_start = pad_end - padded
    tab = jnp.concatenate([pad_start[None, :] + ttab[:, 0], ttab[:, 2], ttab[:, 1] // MOE_CHUNK,
                           jnp.zeros((nt, LANES - 3 * N_EXPERTS), jnp.int32)], axis=1)
    tail = jnp.concatenate([pad_start + counts, (padded - counts) // MOE_CHUNK, pad_end[-1:] // tmx])
    n_rows = t * TOP_K + nt * N_EXPERTS * MOE_CHUNK + N_EXPERTS * tmx
    nb = n_rows // tmx
    block_start = jnp.arange(nb, dtype=jnp.int32) * tmx
    block_expert = jnp.minimum(jnp.sum(pad_end[None, :] <= block_start[:, None], axis=1),
                               N_EXPERTS - 1).astype(jnp.int32)
    n_active = (pad_end[-1:] // tmx).astype(jnp.int32)

    xg = _dispatch(tab, tail, metat, xn, n_rows)
    y = _experts(block_expert, n_active, xg, w1[0], w3[0], w2[0])
    out = _combine(tab, meta, xmid, y)
    return out.reshape(batch, seq, d)
```

```python
import functools
import math

import jax
import jax.numpy as jnp
from jax import lax
from jax.experimental import pallas as pl
from jax.experimental.pallas import tpu as pltpu

F32 = jnp.float32
BF16 = jnp.bfloat16

D_MODEL = 1024
HG_HEADS = 4
HG_DK = 128
HG_WIDTH = HG_HEADS * HG_DK
HG_CHUNK = 64
DA_HEADS = 4
DA_HEAD = 64
DA_VDIM = 2 * DA_HEAD
DA_WIDTH = DA_HEADS * DA_VDIM
ROPE_THETA = 10000.0
N_GROUPS = 4
EXPERTS_PER_GROUP = 8
N_EXPERTS = N_GROUPS * EXPERTS_PER_GROUP
TOP_K = 2
D_FF = 512
EPS = 1e-6
LAM_INIT = 0.8 - 0.6 * math.exp(-0.3 * 0)
IN_COLS = 4 * HG_WIDTH + 3 * DA_WIDTH + 2 * D_MODEL

LANES = 128
SUBLANES = 8
BF16_ROWS = 16
NEG = -1e30

TM_PROJ = 512
TC_HGRN = 512
HGRN_GROUP = 256
TQ_ATTN = 512
TM_MOE = 512
TM_EXPERT = 512
MOE_CHUNK = BF16_ROWS
MOE_LOCAL = TOP_K * TM_MOE + N_EXPERTS * MOE_CHUNK
TAB_COUNT = MOE_LOCAL // MOE_CHUNK
MOE_DMA_GROUP = 8
VMEM_LIMIT = 56 * 1024 * 1024


def _cparams(sem):
    return pltpu.CompilerParams(dimension_semantics=sem, vmem_limit_bytes=VMEM_LIMIT)


def _sigmoid(v):
    return 1.0 / (1.0 + jnp.exp(-v))


def _qk_prep(d, gain, cos, sin, gmat):
    ss = jnp.dot((d * d).astype(BF16), gmat, preferred_element_type=F32)
    y = d * lax.rsqrt(ss * (1.0 / DA_HEAD) + EPS) * gain
    lane = lax.broadcasted_iota(jnp.int32, (d.shape[0], LANES), 1)
    upper = (lane & (DA_HEAD // 2)) != 0
    outs = []
    for c in range(d.shape[1] // LANES):
        yc = y[:, c * LANES:(c + 1) * LANES]
        sw = jnp.where(upper, pltpu.roll(yc, DA_HEAD // 2, 1), pltpu.roll(yc, LANES - DA_HEAD // 2, 1))
        outs.append(yc * cos + sw * sin)
    return jnp.concatenate(outs, axis=1)


def _in_proj_kernel(x_ref, nm_ref, w_ref, lb_ref, gq_ref, gk_ref, cos_ref, sin_ref,
                    hq_ref, g_ref, kk_ref, hi_ref, og_ref, qn_ref, kn_ref, dv_ref, sgh_ref, sgd_ref):
    x = x_ref[...]
    h = x * lax.rsqrt(jnp.mean(x * x, axis=-1, keepdims=True) + EPS) * nm_ref[...]
    hb = h.astype(BF16)

    def proj(c0, width):
        return jnp.dot(hb, w_ref[:, c0:c0 + width], preferred_element_type=F32)

    hq_ref[...] = proj(0, HG_WIDTH).astype(BF16)

    hf = proj(HG_WIDTH, HG_WIDTH)
    lbp = lb_ref[...]
    mx = jnp.maximum(lbp[0:1], lbp[1:2])
    e0 = jnp.exp(lbp[0:1] - mx)
    e1 = jnp.exp(lbp[1:2] - mx)
    lb = e0 / (e0 + e1)
    f = lb + (1.0 - lb) * _sigmoid(hf)
    g_ref[...] = jnp.log(f)
    kk_ref[...] = (1.0 - f).astype(BF16)

    hi_ref[...] = proj(2 * HG_WIDTH, HG_WIDTH).astype(BF16)
    hog = proj(3 * HG_WIDTH, HG_WIDTH)
    og_ref[...] = (hog * _sigmoid(hog)).astype(BF16)

    r = lax.broadcasted_iota(jnp.int32, (DA_WIDTH, DA_WIDTH), 0) // DA_HEAD
    c = lax.broadcasted_iota(jnp.int32, (DA_WIDTH, DA_WIDTH), 1) // DA_HEAD
    gmat = jnp.where(r == c, 1.0, 0.0).astype(BF16)
    cos = cos_ref[...]
    sin = sin_ref[...]
    base = 4 * HG_WIDTH
    qn_ref[...] = _qk_prep(proj(base, DA_WIDTH), gq_ref[...], cos, sin, gmat).astype(BF16)
    kn_ref[...] = _qk_prep(proj(base + DA_WIDTH, DA_WIDTH), gk_ref[...], cos, sin, gmat).astype(BF16)
    dv_ref[...] = proj(base + 2 * DA_WIDTH, DA_WIDTH).astype(BF16)
    base += 3 * DA_WIDTH
    sgh_ref[...] = _sigmoid(proj(base, D_MODEL)).astype(BF16)
    sgd_ref[...] = _sigmoid(proj(base + D_MODEL, D_MODEL)).astype(BF16)


def _in_proj(x2, norm_mix, w_in_bf, hg_lb, gq, gk, cos_t, sin_t, seq):
    t = x2.shape[0]
    tm = TM_PROJ
    nseq = seq // tm
    row = lambda w: pl.BlockSpec((tm, w), lambda i: (i, 0))
    const = lambda shape: pl.BlockSpec(shape, lambda i: (0, 0))
    tab = pl.BlockSpec((tm, LANES), lambda i: (i % nseq, 0))
    out_shape = [
        jax.ShapeDtypeStruct((t, HG_WIDTH), BF16),
        jax.ShapeDtypeStruct((t, HG_WIDTH), F32),
        jax.ShapeDtypeStruct((t, HG_WIDTH), BF16),
        jax.ShapeDtypeStruct((t, HG_WIDTH), BF16),
        jax.ShapeDtypeStruct((t, HG_WIDTH), BF16),
        jax.ShapeDtypeStruct((t, DA_WIDTH), BF16),
        jax.ShapeDtypeStruct((t, DA_WIDTH), BF16),
        jax.ShapeDtypeStruct((t, DA_WIDTH), BF16),
        jax.ShapeDtypeStruct((t, D_MODEL), BF16),
        jax.ShapeDtypeStruct((t, D_MODEL), BF16),
    ]
    out_specs = [row(HG_WIDTH)] * 5 + [row(DA_WIDTH)] * 3 + [row(D_MODEL)] * 2
    return pl.pallas_call(
        _in_proj_kernel,
        grid=(t // tm,),
        in_specs=[row(D_MODEL), const((1, D_MODEL)), const((D_MODEL, IN_COLS)), const((2, HG_WIDTH)),
                  const((1, DA_WIDTH)), const((1, DA_WIDTH)), tab, tab],
        out_specs=out_specs,
        out_shape=out_shape,
        compiler_params=_cparams(("parallel",)),
        name="in_proj",
    )(x2, norm_mix, w_in_bf, hg_lb, gq, gk, cos_t, sin_t)


def _hgrn_kernel(q_ref, g_ref, k_ref, v_ref, og_ref, gain_ref, o_ref, st_ref):
    @pl.when(pl.program_id(1) == 0)
    def _():
        st_ref[...] = jnp.zeros_like(st_ref)

    c = HG_CHUNK
    tc = q_ref.shape[0]
    nc = tc // c
    grp = HGRN_GROUP
    row = lax.broadcasted_iota(jnp.int32, (grp, grp), 0)
    col = lax.broadcasted_iota(jnp.int32, (grp, grp), 1)
    causal = (row // c == col // c) & (col <= row)
    tri = jnp.where(causal, 1.0, 0.0).astype(BF16)
    gain = gain_ref[...]
    nt = (((1,), (1,)), ((), ()))
    pair_lane = lax.broadcasted_iota(jnp.int32, (HG_DK, 2 * c), 1)
    in_chunk = [pair_lane < c, pair_lane >= c]

    g = g_ref[...]
    g_hi = g.astype(BF16)
    g_lo = (g - g_hi.astype(F32)).astype(BF16)
    b = jnp.concatenate(
        [jnp.dot(tri, g_hi[r0:r0 + grp], preferred_element_type=F32)
         + jnp.dot(tri, g_lo[r0:r0 + grp], preferred_element_type=F32) for r0 in range(0, tc, grp)], axis=0)
    b_mid_rows = [b[ci * c + c // 2 - 1:ci * c + c // 2, :] for ci in range(nc)]
    b_end_rows = [b[ci * c + c - 1:ci * c + c, :] for ci in range(nc)]
    per_chunk = lambda rows: jnp.concatenate([jnp.broadcast_to(r, (c, r.shape[1])) for r in rows], axis=0)
    b_mid = per_chunk(b_mid_rows)
    e_fwd = jnp.exp(b - b_mid)
    e_bwd = jnp.exp(b_mid - b)
    q_rel32 = q_ref[...].astype(F32) * e_fwd
    k_rel32 = k_ref[...].astype(F32) * e_bwd
    q_rel = q_rel32.astype(BF16)
    k_rel = k_rel32.astype(BF16)
    q_st = (q_rel32 * per_chunk([jnp.exp(r) for r in b_mid_rows])).astype(BF16)
    k_end = (k_rel32 * per_chunk([jnp.exp(e - m) for e, m in zip(b_end_rows, b_mid_rows)])).astype(BF16)
    decay = [jnp.exp(r) for r in b_end_rows]
    v = v_ref[...]
    og = og_ref[...]

    for h in range(HG_HEADS):
        cols = slice(h * HG_DK, (h + 1) * HG_DK)
        intra = []
        for r0 in range(0, tc, grp):
            rows = slice(r0, r0 + grp)
            sc = lax.dot_general(q_rel[rows, cols], k_rel[rows, cols], nt, preferred_element_type=F32)
            sc = jnp.where(causal, sc, 0.0).astype(BF16)
            intra.append(jnp.dot(sc, v[rows, cols], preferred_element_type=F32))
        v_t = v[:, cols].astype(F32).T.astype(BF16)
        incr = []
        for ci in range(nc):
            pair = slice((ci // 2) * 2 * c, (ci // 2 + 1) * 2 * c)
            lhs = jnp.where(in_chunk[ci % 2], v_t[:, pair], jnp.zeros((HG_DK, 2 * c), BF16))
            incr.append(jnp.dot(lhs, k_end[pair, cols], preferred_element_type=F32))
        st = st_ref[h]
        inter = []
        for ci in range(nc):
            rows = slice(ci * c, (ci + 1) * c)
            inter.append(lax.dot_general(q_st[rows, cols], st.astype(BF16), nt, preferred_element_type=F32))
            st = st * decay[ci][:, cols] + incr[ci]
        st_ref[h] = st
        o = jnp.concatenate(intra, axis=0) + jnp.concatenate(inter, axis=0)
        on = o * lax.rsqrt(jnp.mean(o * o, axis=-1, keepdims=True) + EPS) * gain
        o_ref[:, cols] = (on * og[:, cols].astype(F32)).astype(BF16)


def _hgrn(hq, g, kk, hi, og, gain, batch, seq):
    t = hq.shape[0]
    tc = TC_HGRN
    ns = seq // tc
    spec = pl.BlockSpec((tc, HG_WIDTH), lambda b, s: (b * ns + s, 0))
    return pl.pallas_call(
        _hgrn_kernel,
        grid=(batch, ns),
        in_specs=[spec] * 5 + [pl.BlockSpec((1, HG_DK), lambda b, s: (0, 0))],
        out_specs=spec,
        out_shape=jax.ShapeDtypeStruct((t, HG_WIDTH), BF16),
        scratch_shapes=[pltpu.VMEM((HG_HEADS, HG_DK, HG_DK), F32)],
        compiler_params=_cparams(("parallel", "arbitrary")),
        name="hgrn",
    )(hq, g, kk, hi, og, gain)


def _attn_kernel(lam_ref, q_ref, k_ref, v_ref, gain_ref, o_ref, s_ref, sd_ref, m_ref, acc_ref):
    tq = TQ_ATTN
    dv = DA_VDIM
    nq = q_ref.shape[0] // tq
    nt = (((1,), (1,)), ((), ()))
    ones = jnp.ones((tq, dv), BF16)
    lane = lax.broadcasted_iota(jnp.int32, (tq, dv), 1)
    zero = jnp.zeros((tq, dv), BF16)
    r = lax.broadcasted_iota(jnp.int32, (2 * tq, tq), 0)
    c = lax.broadcasted_iota(jnp.int32, (2 * tq, tq), 1)
    on_or_below_diag = c <= jnp.where(r >= tq, r - tq, r)
    lp = lam_ref[...]
    lam = (jnp.exp(jnp.sum(lp[0:1] * lp[1:2], axis=1, keepdims=True))
           - jnp.exp(jnp.sum(lp[2:3] * lp[3:4], axis=1, keepdims=True)) + LAM_INIT)
    out_gain = gain_ref[...] * (1.0 - LAM_INIT)

    def block_rows(j):
        return pl.ds(pl.multiple_of(j * tq, tq), tq)

    def stacked_q(i):
        q = q_ref[block_rows(i), :]
        return jnp.concatenate([jnp.where(lane < DA_HEAD, q, zero), jnp.where(lane >= DA_HEAD, q, zero)], axis=0)

    def scores(qq, j):
        return lax.dot_general(qq, k_ref[block_rows(j), :], nt, preferred_element_type=F32)

    def diag_scores(qq, i):
        return jnp.where(on_or_below_diag, scores(qq, i), NEG)

    def accumulate(s, j):
        v1 = jnp.concatenate([v_ref[block_rows(j), :], ones], axis=1)
        m_prev = m_ref[...]
        m_new = jnp.maximum(m_prev, jnp.max(s, axis=1, keepdims=True))
        alpha = jnp.exp2(m_prev - m_new)
        p = jnp.exp2(s - jnp.tile(m_new, (1, tq // LANES)))
        acc_ref[...] = (jnp.tile(alpha, (1, 2)) * acc_ref[...]
                        + jnp.dot(p.astype(BF16), v1, preferred_element_type=F32))
        m_ref[...] = m_new

    def reset():
        m_ref[...] = jnp.full_like(m_ref, NEG)
        acc_ref[...] = jnp.zeros_like(acc_ref)

    def finish(i):
        accumulate(sd_ref[...], i)
        acc = acc_ref[...]
        o = acc[:tq, :dv] / acc[:tq, dv:] - lam * (acc[tq:, :dv] / acc[tq:, dv:])
        on = o * lax.rsqrt(jnp.mean(o * o, axis=-1, keepdims=True) + EPS) * out_gain
        o_ref[block_rows(i), :] = on.astype(BF16)

    reset()
    sd_ref[...] = diag_scores(stacked_q(0), 0)

    def q_block(i, carry):
        qq = stacked_q(i)
        s_ref[0] = scores(qq, 0)
        finish(i - 1)
        reset()

        def pair(m, carry2):
            s_ref[1] = scores(qq, 2 * m + 1)
            accumulate(s_ref[0], 2 * m)
            s_ref[0] = scores(qq, 2 * m + 2)
            accumulate(s_ref[1], 2 * m + 1)
            return carry2

        lax.fori_loop(0, (i - 1) // 2, pair, 0)

        @pl.when(i % 2 == 0)
        def _():
            s_ref[1] = scores(qq, i - 1)
            accumulate(s_ref[0], i - 2)

        sd_ref[...] = diag_scores(qq, i)
        accumulate(s_ref[(i - 1) % 2], i - 1)
        return carry

    lax.fori_loop(1, nq, q_block, 0)
    finish(nq - 1)


def _attn(lam_p, qn, kn, dv, gain, batch, seq):
    t = qn.shape[0]
    tq = TQ_ATTN
    head = lambda b, h: (b, h)
    return pl.pallas_call(
        _attn_kernel,
        grid=(batch, DA_HEADS),
        in_specs=[pl.BlockSpec((4, DA_HEAD), lambda b, h: (0, 0)),
                  pl.BlockSpec((seq, DA_VDIM), head),
                  pl.BlockSpec((seq, DA_VDIM), head),
                  pl.BlockSpec((seq, DA_VDIM), head),
                  pl.BlockSpec((1, DA_VDIM), lambda b, h: (0, 0))],
        out_specs=pl.BlockSpec((seq, DA_VDIM), head),
        out_shape=jax.ShapeDtypeStruct((t, DA_WIDTH), BF16),
        scratch_shapes=[pltpu.VMEM((2, 2 * tq, tq), F32), pltpu.VMEM((2 * tq, tq), F32),
                        pltpu.VMEM((2 * tq, LANES), F32), pltpu.VMEM((2 * tq, 2 * DA_VDIM), F32)],
        compiler_params=_cparams(("parallel", "parallel")),
        name="attn",
    )(lam_p, qn, kn, dv, gain)


def _merge_kernel(ohg_ref, oda_ref, sgh_ref, sgd_ref, x_ref, wbh_ref, wbd_ref, wo_ref, nm_ref, wr_ref, br_ref,
                  xmid_ref, xn_ref, meta_ref, metat_ref, ttab_ref, cnt_ref, carry_ref):
    tm = x_ref.shape[0]

    @pl.when(pl.program_id(0) == 0)
    def _():
        carry_ref[...] = jnp.zeros_like(carry_ref)

    y_hg = jnp.dot(ohg_ref[...], wbh_ref[...], preferred_element_type=F32)
    y_da = jnp.dot(oda_ref[...], wbd_ref[...], preferred_element_type=F32)
    mixed = sgh_ref[...].astype(F32) * y_hg + sgd_ref[...].astype(F32) * y_da
    xm = x_ref[...] + jnp.dot(mixed.astype(BF16), wo_ref[...], preferred_element_type=F32)
    xmid_ref[...] = xm
    xn = xm * lax.rsqrt(jnp.mean(xm * xm, axis=-1, keepdims=True) + EPS) * nm_ref[...]
    xn_ref[...] = xn.astype(BF16)

    logit = jnp.dot(xn.astype(BF16), wr_ref[...].astype(BF16), preferred_element_type=F32) + br_ref[...]

    lane = lax.broadcasted_iota(jnp.int32, (tm, LANES), 1)
    lanef = lane.astype(F32)
    big = float(LANES)
    is_g = lane < N_GROUPS
    gl = jnp.where(is_g, logit, NEG)
    gmax = jnp.max(gl, axis=1, keepdims=True)
    g_sel = jnp.min(jnp.where(gl == gmax, lanef, big), axis=1, keepdims=True)
    g_w = 1.0 / jnp.sum(jnp.where(is_g, jnp.exp(gl - gmax), 0.0), axis=1, keepdims=True)
    lo = N_GROUPS + EXPERTS_PER_GROUP * g_sel
    in_grp = (lanef >= lo) & (lanef < lo + EXPERTS_PER_GROUP)
    el = jnp.where(in_grp, logit, NEG)
    v1 = jnp.max(el, axis=1, keepdims=True)
    i1 = jnp.min(jnp.where(el == v1, lanef, big), axis=1, keepdims=True)
    el2 = jnp.where(lanef == i1, NEG, el)
    v2 = jnp.max(el2, axis=1, keepdims=True)
    i2 = jnp.min(jnp.where(el2 == v2, lanef, big), axis=1, keepdims=True)
    e1 = i1 - N_GROUPS
    e2 = i2 - N_GROUPS
    d = jnp.exp(v2 - v1)
    w1 = g_w / (1.0 + d)
    w2 = g_w * d / (1.0 + d)

    oh1 = jnp.where(lanef == e1, 1.0, 0.0)
    oh2 = jnp.where(lanef == e2, 1.0, 0.0)
    r = lax.broadcasted_iota(jnp.int32, (tm, tm), 0)
    c = lax.broadcasted_iota(jnp.int32, (tm, tm), 1)
    below = jnp.where(c < r, 1.0, 0.0).astype(BF16)
    c1 = jnp.dot(below, oh1.astype(BF16), preferred_element_type=F32)
    c2 = jnp.dot(below, oh2.astype(BF16), preferred_element_type=F32)
    tot1 = jnp.sum(oh1, axis=0, keepdims=True)
    tot2 = jnp.sum(oh2, axis=0, keepdims=True)
    n_chunk = jnp.floor((tot1 + tot2 + (MOE_CHUNK - 1)) * (1.0 / MOE_CHUNK))
    r = lax.broadcasted_iota(jnp.int32, (LANES, LANES), 0)
    c = lax.broadcasted_iota(jnp.int32, (LANES, LANES), 1)
    before = jnp.where(r < c, 1.0, 0.0).astype(BF16)
    lstart = MOE_CHUNK * jnp.dot(jnp.broadcast_to(n_chunk, (SUBLANES, LANES)).astype(BF16), before,
                                 preferred_element_type=F32)[0:1]
    lo1 = jnp.sum(oh1 * (c1 + lstart), axis=1, keepdims=True)
    lo2 = jnp.sum(oh2 * (c2 + lstart + tot1), axis=1, keepdims=True)
    carry = carry_ref[...]
    ttab_ref[...] = jnp.concatenate([carry, MOE_CHUNK * n_chunk, lstart, jnp.zeros((SUBLANES - 3, LANES), F32)], axis=0)
    carry = carry + MOE_CHUNK * n_chunk
    carry_ref[...] = carry
    cnt_ref[...] = jnp.broadcast_to(carry, cnt_ref.shape)

    meta = jnp.where(lane == 0, e1, jnp.where(lane == 1, e2, jnp.where(lane == 2, lo1, jnp.where(
        lane == 3, lo2, jnp.where(lane == 4, w1, jnp.where(lane == 5, w2, 0.0))))))
    meta_ref[...] = meta
    metat_ref[...] = meta.T[0:SUBLANES, :]


def _merge(o_hg, o_da, sgh, sgd, x2, wbh, wbd, wo, norm_moe, wr, br):
    t = x2.shape[0]
    tm = TM_MOE
    nt = t // tm
    row = lambda w: pl.BlockSpec((tm, w), lambda i: (i, 0))
    const = lambda shape: pl.BlockSpec(shape, lambda i: (0, 0))
    return pl.pallas_call(
        _merge_kernel,
        grid=(nt,),
        in_specs=[row(HG_WIDTH), row(DA_WIDTH), row(D_MODEL), row(D_MODEL), row(D_MODEL),
                  const((HG_WIDTH, D_MODEL)), const((DA_WIDTH, D_MODEL)), const((D_MODEL, D_MODEL)),
                  const((1, D_MODEL)), const((D_MODEL, LANES)), const((1, LANES))],
        out_specs=[row(D_MODEL), row(D_MODEL), row(LANES),
                   pl.BlockSpec((SUBLANES, tm), lambda i: (0, i)),
                   pl.BlockSpec((SUBLANES, LANES), lambda i: (i, 0)),
                   const((SUBLANES, LANES))],
        out_shape=[jax.ShapeDtypeStruct((t, D_MODEL), F32),
                   jax.ShapeDtypeStruct((t, D_MODEL), BF16),
                   jax.ShapeDtypeStruct((t, LANES), F32),
                   jax.ShapeDtypeStruct((SUBLANES, t), F32),
                   jax.ShapeDtypeStruct((nt * SUBLANES, LANES), F32),
                   jax.ShapeDtypeStruct((SUBLANES, LANES), F32)],
        scratch_shapes=[pltpu.VMEM((1, LANES), F32)],
        compiler_params=_cparams(("arbitrary",)),
        name="merge",
    )(o_hg, o_da, sgh, sgd, x2, wbh, wbd, wo, norm_moe, wr, br)


def _for_each_chunk(tab_ref, tile, fn, group=MOE_DMA_GROUP):
    def one(c):
        fn(pl.multiple_of(c * MOE_CHUNK, MOE_CHUNK), pl.multiple_of(tab_ref[tile, c], MOE_CHUNK))

    def per_group(g, carry):
        for u in range(group):
            one(g * group + u)
        return carry

    def per_chunk(c, carry):
        one(c)
        return carry

    n = tab_ref[tile, TAB_COUNT]
    n_grouped = lax.div(n, group) * group
    lax.fori_loop(0, lax.div(n, group), per_group, 0)
    lax.fori_loop(n_grouped, n, per_chunk, 0)


def _dispatch_kernel(tab_ref, tail_ref, metat_ref, xn_ref, xg_hbm, xp_ref, zero_ref, sem, sem_z):
    i = pl.program_id(0)
    last = pl.num_programs(0) - 1
    lo1 = metat_ref[2:3, :]
    lo2 = metat_ref[3:4, :]
    r = lax.broadcasted_iota(jnp.int32, (MOE_LOCAL, xn_ref.shape[0]), 0).astype(F32)
    perm = jnp.where((r == lo1) | (r == lo2), 1.0, 0.0).astype(BF16)
    xp_ref[i % 2] = jnp.dot(perm, xn_ref[...], preferred_element_type=F32).astype(BF16)

    def chunk_copy(tile, lrow, grow):
        return pltpu.make_async_copy(xp_ref.at[tile % 2, pl.ds(lrow, MOE_CHUNK), :],
                                     xg_hbm.at[pl.ds(grow, MOE_CHUNK), :], sem.at[tile % 2])

    _for_each_chunk(tab_ref, i, lambda lrow, grow: chunk_copy(i, lrow, grow).start())

    @pl.when(i > 0)
    def _():
        _for_each_chunk(tab_ref, i - 1, lambda lrow, grow: chunk_copy(i - 1, lrow, grow).wait())

    def tail_copy(e, c):
        grow = pl.multiple_of(tail_ref[e] + c * MOE_CHUNK, MOE_CHUNK)
        return pltpu.make_async_copy(zero_ref.at[pl.ds(0, MOE_CHUNK), :], xg_hbm.at[pl.ds(grow, MOE_CHUNK), :], sem_z)

    def for_each_tail(fn):
        def per_expert(e, carry):
            lax.fori_loop(0, tail_ref[N_EXPERTS + e], lambda c, c2: (fn(e, c), c2)[1], 0)
            return carry
        lax.fori_loop(0, N_EXPERTS, per_expert, 0)

    def block_copy(b):
        grow = pl.multiple_of(b * TM_EXPERT, TM_EXPERT)
        return pltpu.make_async_copy(zero_ref, xg_hbm.at[pl.ds(grow, TM_EXPERT), :], sem_z)

    def for_each_free_block(fn):
        lax.fori_loop(tail_ref[2 * N_EXPERTS], xg_hbm.shape[0] // TM_EXPERT, lambda b, c2: (fn(b), c2)[1], 0)

    @pl.when(i == last)
    def _():
        zero_ref[...] = jnp.zeros_like(zero_ref)
        for_each_tail(lambda e, c: tail_copy(e, c).start())
        for_each_free_block(lambda b: block_copy(b).start())
        for_each_tail(lambda e, c: tail_copy(e, c).wait())
        for_each_free_block(lambda b: block_copy(b).wait())
        _for_each_chunk(tab_ref, i, lambda lrow, grow: chunk_copy(i, lrow, grow).wait())


def _dispatch(tab, tail, metat, xn, n_rows):
    t = xn.shape[0]
    tm = TM_MOE
    grid_spec = pltpu.PrefetchScalarGridSpec(
        num_scalar_prefetch=2,
        grid=(t // tm,),
        in_specs=[pl.BlockSpec((SUBLANES, tm), lambda i, tab, tail: (0, i)),
                  pl.BlockSpec((tm, D_MODEL), lambda i, tab, tail: (i, 0))],
        out_specs=pl.BlockSpec(memory_space=pl.ANY),
        scratch_shapes=[pltpu.VMEM((2, MOE_LOCAL, D_MODEL), BF16), pltpu.VMEM((TM_EXPERT, D_MODEL), BF16),
                        pltpu.SemaphoreType.DMA((2,)), pltpu.SemaphoreType.DMA],
    )
    return pl.pallas_call(
        _dispatch_kernel,
        grid_spec=grid_spec,
        out_shape=jax.ShapeDtypeStruct((n_rows, D_MODEL), BF16),
        compiler_params=_cparams(("arbitrary",)),
        name="dispatch",
    )(tab, tail, metat, xn)


def _expert_kernel(be_ref, nact_ref, xg_ref, w1_ref, w3_ref, w2_ref, y_ref, w1b_ref, w3b_ref, w2b_ref):
    i = pl.program_id(0)
    active = i < nact_ref[0]

    @pl.when(active & ((i == 0) | (be_ref[i] != be_ref[jnp.maximum(i - 1, 0)])))
    def _():
        w1b_ref[...] = w1_ref[0].astype(BF16)
        w3b_ref[...] = w3_ref[0].astype(BF16)
        w2b_ref[...] = w2_ref[0].astype(BF16)

    @pl.when(active)
    def _():
        xb = xg_ref[...]
        a = jnp.dot(xb, w1b_ref[...], preferred_element_type=F32)
        b = jnp.dot(xb, w3b_ref[...], preferred_element_type=F32)
        hid = (a * _sigmoid(a) * b).astype(BF16)
        y_ref[...] = jnp.dot(hid, w2b_ref[...], preferred_element_type=F32).astype(BF16)

    @pl.when(jnp.logical_not(active))
    def _():
        y_ref[...] = jnp.zeros_like(y_ref)


def _experts(block_expert, n_active, xg, w1, w3, w2):
    tm = TM_EXPERT
    nb = xg.shape[0] // tm
    blk = lambda i, be, na: jnp.minimum(i, na[0] - 1)
    grid_spec = pltpu.PrefetchScalarGridSpec(
        num_scalar_prefetch=2,
        grid=(nb,),
        in_specs=[pl.BlockSpec((tm, D_MODEL), lambda i, be, na: (blk(i, be, na), 0)),
                  pl.BlockSpec((1, D_MODEL, D_FF), lambda i, be, na: (be[blk(i, be, na)], 0, 0)),
                  pl.BlockSpec((1, D_MODEL, D_FF), lambda i, be, na: (be[blk(i, be, na)], 0, 0)),
                  pl.BlockSpec((1, D_FF, D_MODEL), lambda i, be, na: (be[blk(i, be, na)], 0, 0))],
        out_specs=pl.BlockSpec((tm, D_MODEL), lambda i, be, na: (i, 0)),
        scratch_shapes=[pltpu.VMEM((D_MODEL, D_FF), BF16), pltpu.VMEM((D_MODEL, D_FF), BF16),
                        pltpu.VMEM((D_FF, D_MODEL), BF16)],
    )
    return pl.pallas_call(
        _expert_kernel,
        grid_spec=grid_spec,
        out_shape=jax.ShapeDtypeStruct(xg.shape, BF16),
        compiler_params=_cparams(("arbitrary",)),
        name="experts",
    )(block_expert, n_active, xg, w1, w3, w2)


def _combine_kernel(tab_ref, meta_ref, xmid_ref, y_hbm, o_ref, yl_ref, sem):
    tm = xmid_ref.shape[0]
    i = pl.program_id(0)

    def chunk_copy(tile, lrow, grow):
        return pltpu.make_async_copy(y_hbm.at[pl.ds(grow, MOE_CHUNK), :],
                                     yl_ref.at[tile % 2, pl.ds(lrow, MOE_CHUNK), :], sem.at[tile % 2])

    @pl.when(i == 0)
    def _():
        yl_ref[...] = jnp.zeros_like(yl_ref)
        _for_each_chunk(tab_ref, i, lambda lrow, grow: chunk_copy(i, lrow, grow).start())

    @pl.when(i + 1 < pl.num_programs(0))
    def _():
        _for_each_chunk(tab_ref, i + 1, lambda lrow, grow: chunk_copy(i + 1, lrow, grow).start())

    meta = meta_ref[...]
    lane = lax.broadcasted_iota(jnp.int32, (tm, MOE_LOCAL), 1).astype(F32)
    wsel = (jnp.where(lane == meta[:, 2:3], meta[:, 4:5], 0.0)
            + jnp.where(lane == meta[:, 3:4], meta[:, 5:6], 0.0)).astype(BF16)

    _for_each_chunk(tab_ref, i, lambda lrow, grow: chunk_copy(i, lrow, grow).wait())
    o_ref[...] = xmid_ref[...] + jnp.dot(wsel, yl_ref[i % 2], preferred_element_type=F32)


def _combine(tab, meta, xmid, y):
    t = xmid.shape[0]
    tm = TM_MOE
    grid_spec = pltpu.PrefetchScalarGridSpec(
        num_scalar_prefetch=1,
        grid=(t // tm,),
        in_specs=[pl.BlockSpec((tm, LANES), lambda i, tab: (i, 0)),
                  pl.BlockSpec((tm, D_MODEL), lambda i, tab: (i, 0)),
                  pl.BlockSpec(memory_space=pl.ANY)],
        out_specs=pl.BlockSpec((tm, D_MODEL), lambda i, tab: (i, 0)),
        scratch_shapes=[pltpu.VMEM((2, MOE_LOCAL, D_MODEL), BF16), pltpu.SemaphoreType.DMA((2,))],
    )
    return pl.pallas_call(
        _combine_kernel,
        grid_spec=grid_spec,
        out_shape=jax.ShapeDtypeStruct((t, D_MODEL), F32),
        compiler_params=_cparams(("arbitrary",)),
        name="combine",
    )(tab, meta, xmid, y)


def kernel(x, norm_mix, w_in, hg_lb, hg_out_norm, da_q_norm, da_k_norm, da_lambda, da_out_norm, w_branch_hg,
           w_branch_da, w_out, norm_moe, w_router_group, b_router_group, w_router_expert, b_router_expert,
           w1, w3, w2):
    batch, seq, d = x.shape
    assert d == D_MODEL and norm_mix.shape[0] == 1 and w_in.shape[2] == IN_COLS
    assert seq % TC_HGRN == 0 and seq % TQ_ATTN == 0 and (batch * seq) % TM_MOE == 0
    t = batch * seq
    x2 = x.reshape(t, d)

    half = DA_HEAD // 2
    inv = ROPE_THETA ** (-jnp.arange(half, dtype=F32) / half)
    ang = jnp.arange(seq, dtype=F32)[:, None] * inv[None, :]
    cos_t = jnp.tile(jnp.cos(ang), (1, 2 * LANES // DA_HEAD))
    sin_t = jnp.tile(jnp.concatenate([-jnp.sin(ang), jnp.sin(ang)], axis=1), (1, LANES // DA_HEAD))

    reps = DA_WIDTH // DA_HEAD
    gq = jnp.tile(da_q_norm[0].astype(F32) * (DA_HEAD ** -0.5 * math.log2(math.e)), reps)[None, :]
    gk = jnp.tile(da_k_norm[0].astype(F32), reps)[None, :]

    hq, g, kk, hi, og, qn, kn, dv, sgh, sgd = _in_proj(
        x2, norm_mix.astype(F32), w_in[0].astype(BF16), hg_lb.astype(F32), gq, gk, cos_t, sin_t, seq)

    o_hg = _hgrn(hq, g, kk, hi, og, hg_out_norm.astype(F32), batch, seq)
    o_da = _attn(da_lambda[0].astype(F32), qn, kn, dv, da_out_norm.astype(F32), batch, seq)

    wr = jnp.zeros((D_MODEL, LANES), F32)
    wr = wr.at[:, :N_GROUPS].set(w_router_group[0]).at[:, N_GROUPS:N_GROUPS + N_EXPERTS].set(w_router_expert[0])
    br = jnp.zeros((1, LANES), F32)
    br = br.at[0, :N_GROUPS].set(b_router_group[0]).at[0, N_GROUPS:N_GROUPS + N_EXPERTS].set(b_router_expert[0])

    xmid, xn, meta, metat, ttab, cnt = _merge(
        o_hg, o_da, sgh, sgd, x2, w_branch_hg[0].astype(BF16), w_branch_da[0].astype(BF16), w_out[0].astype(BF16),
        norm_moe.astype(F32), wr, br)

    tmx = TM_EXPERT
    nt = t // TM_MOE
    ttab = ttab.reshape(nt, SUBLANES, LANES)[:, :, :N_EXPERTS].astype(jnp.int32)
    counts = cnt[0, :N_EXPERTS].astype(jnp.int32)
    padded = (counts + tmx - 1) // tmx * tmx
    pad_end = jnp.cumsum(padded)
    pad_start = pad_end - padded
    gstart, seg_len, lstart = pad_start[None, :] + ttab[:, 0], ttab[:, 1], ttab[:, 2]
    lrow = (jnp.arange(TAB_COUNT, dtype=jnp.int32) * MOE_CHUNK)[None, :, None]
    in_seg = (lrow >= lstart[:, None, :]) & (lrow < (lstart + seg_len)[:, None, :])
    chunk_grow = jnp.sum(jnp.where(in_seg, gstart[:, None, :] + lrow - lstart[:, None, :], 0), axis=2)
    tab = jnp.concatenate([chunk_grow, jnp.sum(seg_len, axis=1, keepdims=True) // MOE_CHUNK,
                           jnp.zeros((nt, LANES - TAB_COUNT - 1), jnp.int32)], axis=1)
    tail = jnp.concatenate([pad_start + counts, (padded - counts) // MOE_CHUNK, pad_end[-1:] // tmx])
    n_rows = t * TOP_K + nt * N_EXPERTS * MOE_CHUNK + N_EXPERTS * tmx
    nb = n_rows // tmx
    block_start = jnp.arange(nb, dtype=jnp.int32) * tmx
    block_expert = jnp.minimum(jnp.sum(pad_end[None, :] <= block_start[:, None], axis=1),
                               N_EXPERTS - 1).astype(jnp.int32)
    n_active = (pad_end[-1:] // tmx).astype(jnp.int32)

    xg = _dispatch(tab, tail, metat, xn, n_rows)
    y = _experts(block_expert, n_active, xg, w1[0], w3[0], w2[0])
    out = _combine(tab, meta, xmid, y)
    return out.reshape(batch, seq, d)
```

```python
import functools
import math

import jax
import jax.numpy as jnp
from jax import lax
from jax.experimental import pallas as pl
from jax.experimental.pallas import tpu as pltpu

F32 = jnp.float32
BF16 = jnp.bfloat16

D_MODEL = 1024
HG_HEADS = 4
HG_DK = 128
HG_WIDTH = HG_HEADS * HG_DK
HG_CHUNK = 64
DA_HEADS = 4
DA_HEAD = 64
DA_VDIM = 2 * DA_HEAD
DA_WIDTH = DA_HEADS * DA_VDIM
ROPE_THETA = 10000.0
N_GROUPS = 4
EXPERTS_PER_GROUP = 8
N_EXPERTS = N_GROUPS * EXPERTS_PER_GROUP
TOP_K = 2
D_FF = 512
EPS = 1e-6
LAM_INIT = 0.8 - 0.6 * math.exp(-0.3 * 0)
IN_COLS = 4 * HG_WIDTH + 3 * DA_WIDTH + 2 * D_MODEL

LANES = 128
SUBLANES = 8
BF16_ROWS = 16
NEG = -1e30

TM_PROJ = 512
HGRN_GROUP = 256
TQ_ATTN = 512
ATTN_REGION = 9
TM_MOE = 512
TM_EXPERT = 512
MOE_CHUNK = BF16_ROWS
MOE_LOCAL = TOP_K * TM_MOE + N_EXPERTS * MOE_CHUNK
TAB_COUNT = MOE_LOCAL // MOE_CHUNK
MOE_DMA_GROUP = 8
VMEM_LIMIT = 56 * 1024 * 1024


def _cparams(sem):
    return pltpu.CompilerParams(dimension_semantics=sem, vmem_limit_bytes=VMEM_LIMIT)


def _sigmoid(v):
    return 1.0 / (1.0 + jnp.exp(-v))


def _qk_prep(d, gain, cos, sin, gmat):
    ss = jnp.dot((d * d).astype(BF16), gmat, preferred_element_type=F32)
    y = d * lax.rsqrt(ss * (1.0 / DA_HEAD) + EPS) * gain
    lane = lax.broadcasted_iota(jnp.int32, (d.shape[0], LANES), 1)
    upper = (lane & (DA_HEAD // 2)) != 0
    outs = []
    for c in range(d.shape[1] // LANES):
        yc = y[:, c * LANES:(c + 1) * LANES]
        sw = jnp.where(upper, pltpu.roll(yc, DA_HEAD // 2, 1), pltpu.roll(yc, LANES - DA_HEAD // 2, 1))
        outs.append(yc * cos + sw * sin)
    return jnp.concatenate(outs, axis=1)


def _in_proj_kernel(x_ref, nm_ref, w_ref, lb_ref, gq_ref, gk_ref, cos_ref, sin_ref, hgain_ref,
                    ohg_ref, qn_ref, kn_ref, dv_ref, sgh_ref, sgd_ref, st_ref, *, tiles_per_seq):
    @pl.when(pl.program_id(0) % tiles_per_seq == 0)
    def _():
        st_ref[...] = jnp.zeros_like(st_ref)

    x = x_ref[...]
    h = x * lax.rsqrt(jnp.mean(x * x, axis=-1, keepdims=True) + EPS) * nm_ref[...]
    hb = h.astype(BF16)

    def proj(c0, width):
        return jnp.dot(hb, w_ref[:, c0:c0 + width], preferred_element_type=F32)

    hq = proj(0, HG_WIDTH)
    hf = proj(HG_WIDTH, HG_WIDTH)
    lbp = lb_ref[...]
    mx = jnp.maximum(lbp[0:1], lbp[1:2])
    e0 = jnp.exp(lbp[0:1] - mx)
    e1 = jnp.exp(lbp[1:2] - mx)
    lb = e0 / (e0 + e1)
    f = lb + (1.0 - lb) * _sigmoid(hf)
    hi = proj(2 * HG_WIDTH, HG_WIDTH)
    hog = proj(3 * HG_WIDTH, HG_WIDTH)
    hgrn_head = _hgrn_tile(hq, jnp.log(f), 1.0 - f, hi, hog * _sigmoid(hog), hgain_ref[...], st_ref)

    def put_head(h):
        ohg_ref[:, h * HG_DK:(h + 1) * HG_DK] = hgrn_head(h)

    r = lax.broadcasted_iota(jnp.int32, (DA_WIDTH, DA_WIDTH), 0) // DA_HEAD
    c = lax.broadcasted_iota(jnp.int32, (DA_WIDTH, DA_WIDTH), 1) // DA_HEAD
    gmat = jnp.where(r == c, 1.0, 0.0).astype(BF16)
    cos = cos_ref[...]
    sin = sin_ref[...]
    base = 4 * HG_WIDTH
    qn_ref[...] = _qk_prep(proj(base, DA_WIDTH), gq_ref[...], cos, sin, gmat).astype(BF16)
    put_head(0)
    kn_ref[...] = _qk_prep(proj(base + DA_WIDTH, DA_WIDTH), gk_ref[...], cos, sin, gmat).astype(BF16)
    put_head(1)
    dv_ref[...] = proj(base + 2 * DA_WIDTH, DA_WIDTH).astype(BF16)
    base += 3 * DA_WIDTH
    sgh_ref[...] = _sigmoid(proj(base, D_MODEL)).astype(BF16)
    put_head(2)
    sgd_ref[...] = _sigmoid(proj(base + D_MODEL, D_MODEL)).astype(BF16)
    put_head(3)


def _in_proj(x2, norm_mix, w_in_bf, hg_lb, gq, gk, cos_t, sin_t, hg_gain, seq):
    t = x2.shape[0]
    tm = TM_PROJ
    nseq = seq // tm
    row = lambda w: pl.BlockSpec((tm, w), lambda i: (i, 0))
    const = lambda shape: pl.BlockSpec(shape, lambda i: (0, 0))
    tab = pl.BlockSpec((tm, LANES), lambda i: (i % nseq, 0))
    out_shape = [
        jax.ShapeDtypeStruct((t, HG_WIDTH), BF16),
        jax.ShapeDtypeStruct((t, DA_WIDTH), BF16),
        jax.ShapeDtypeStruct((t, DA_WIDTH), BF16),
        jax.ShapeDtypeStruct((t, DA_WIDTH), BF16),
        jax.ShapeDtypeStruct((t, D_MODEL), BF16),
        jax.ShapeDtypeStruct((t, D_MODEL), BF16),
    ]
    out_specs = [row(HG_WIDTH)] + [row(DA_WIDTH)] * 3 + [row(D_MODEL)] * 2
    return pl.pallas_call(
        functools.partial(_in_proj_kernel, tiles_per_seq=nseq),
        grid=(t // tm,),
        in_specs=[row(D_MODEL), const((1, D_MODEL)), const((D_MODEL, IN_COLS)), const((2, HG_WIDTH)),
                  const((1, DA_WIDTH)), const((1, DA_WIDTH)), tab, tab, const((1, HG_DK))],
        out_specs=out_specs,
        out_shape=out_shape,
        scratch_shapes=[pltpu.VMEM((HG_HEADS, HG_DK, HG_DK), F32)],
        compiler_params=_cparams(("arbitrary",)),
        name="in_proj",
    )(x2, norm_mix, w_in_bf, hg_lb, gq, gk, cos_t, sin_t, hg_gain)


def _hgrn_tile(q, g, k, v32, og, gain, st_ref):
    c = HG_CHUNK
    tc = q.shape[0]
    nc = tc // c
    grp = HGRN_GROUP
    row = lax.broadcasted_iota(jnp.int32, (grp, grp), 0)
    col = lax.broadcasted_iota(jnp.int32, (grp, grp), 1)
    causal = (row // c == col // c) & (col <= row)
    tri = jnp.where(causal, 1.0, 0.0).astype(BF16)
    nt = (((1,), (1,)), ((), ()))
    pair_lane = lax.broadcasted_iota(jnp.int32, (HG_DK, 2 * c), 1)
    in_chunk = [pair_lane < c, pair_lane >= c]

    g_hi = g.astype(BF16)
    g_lo = (g - g_hi.astype(F32)).astype(BF16)
    b = jnp.concatenate(
        [jnp.dot(tri, g_hi[r0:r0 + grp], preferred_element_type=F32)
         + jnp.dot(tri, g_lo[r0:r0 + grp], preferred_element_type=F32) for r0 in range(0, tc, grp)], axis=0)
    b_mid_rows = [b[ci * c + c // 2 - 1:ci * c + c // 2, :] for ci in range(nc)]
    b_end_rows = [b[ci * c + c - 1:ci * c + c, :] for ci in range(nc)]
    per_chunk = lambda rows: jnp.concatenate([jnp.broadcast_to(r, (c, r.shape[1])) for r in rows], axis=0)
    b_mid = per_chunk(b_mid_rows)
    e_fwd = jnp.exp(b - b_mid)
    e_bwd = jnp.exp(b_mid - b)
    q_rel32 = q * e_fwd
    k_rel32 = k * e_bwd
    q_rel = q_rel32.astype(BF16)
    k_rel = k_rel32.astype(BF16)
    q_st = (q_rel32 * per_chunk([jnp.exp(r) for r in b_mid_rows])).astype(BF16)
    k_end = (k_rel32 * per_chunk([jnp.exp(e - m) for e, m in zip(b_end_rows, b_mid_rows)])).astype(BF16)
    decay = [jnp.exp(r) for r in b_end_rows]
    v = v32.astype(BF16)

    def head(h):
        cols = slice(h * HG_DK, (h + 1) * HG_DK)
        intra = []
        for r0 in range(0, tc, grp):
            rows = slice(r0, r0 + grp)
            sc = lax.dot_general(q_rel[rows, cols], k_rel[rows, cols], nt, preferred_element_type=F32)
            sc = jnp.where(causal, sc, 0.0).astype(BF16)
            intra.append(jnp.dot(sc, v[rows, cols], preferred_element_type=F32))
        v_t = v32[:, cols].T.astype(BF16)
        incr = []
        for ci in range(nc):
            pair = slice((ci // 2) * 2 * c, (ci // 2 + 1) * 2 * c)
            lhs = jnp.where(in_chunk[ci % 2], v_t[:, pair], jnp.zeros((HG_DK, 2 * c), BF16))
            incr.append(jnp.dot(lhs, k_end[pair, cols], preferred_element_type=F32))
        st = st_ref[h]
        inter = []
        for ci in range(nc):
            rows = slice(ci * c, (ci + 1) * c)
            inter.append(lax.dot_general(q_st[rows, cols], st.astype(BF16), nt, preferred_element_type=F32))
            st = st * decay[ci][:, cols] + incr[ci]
        st_ref[h] = st
        o = jnp.concatenate(intra, axis=0) + jnp.concatenate(inter, axis=0)
        on = o * lax.rsqrt(jnp.mean(o * o, axis=-1, keepdims=True) + EPS) * gain
        return (on * og[:, cols]).astype(BF16)

    return head


def _attn_kernel(lam_ref, q_ref, k_ref, v_ref, gain_ref, o_ref, s_ref, m_ref, acc_ref):
    tq = TQ_ATTN
    dv = DA_VDIM
    nq = q_ref.shape[0] // tq
    nt = (((1,), (1,)), ((), ()))
    ones = jnp.ones((tq, dv), BF16)
    lane = lax.broadcasted_iota(jnp.int32, (tq, dv), 1)
    zero = jnp.zeros((tq, dv), BF16)
    r = lax.broadcasted_iota(jnp.int32, (2 * tq, tq), 0)
    c = lax.broadcasted_iota(jnp.int32, (2 * tq, tq), 1)
    on_or_below_diag = c <= jnp.where(r >= tq, r - tq, r)
    lp = lam_ref[...]
    lam = (jnp.exp(jnp.sum(lp[0:1] * lp[1:2], axis=1, keepdims=True))
           - jnp.exp(jnp.sum(lp[2:3] * lp[3:4], axis=1, keepdims=True)) + LAM_INIT)
    out_gain = gain_ref[...] * (1.0 - LAM_INIT)

    def block_rows(j):
        return slice(j * tq, (j + 1) * tq)

    def stacked_q(i):
        q = q_ref[block_rows(i), :]
        return jnp.concatenate([jnp.where(lane < DA_HEAD, q, zero), jnp.where(lane >= DA_HEAD, q, zero)], axis=0)

    def accumulate(s, j):
        v1 = jnp.concatenate([v_ref[block_rows(j), :], ones], axis=1)
        m_prev = m_ref[...]
        m_new = jnp.maximum(m_prev, jnp.max(s, axis=1, keepdims=True))
        alpha = jnp.exp2(m_prev - m_new)
        p = jnp.exp2(s - jnp.tile(m_new, (1, tq // LANES)))
        acc_ref[...] = (jnp.tile(alpha, (1, 2)) * acc_ref[...]
                        + jnp.dot(p.astype(BF16), v1, preferred_element_type=F32))
        m_ref[...] = m_new

    visits = [(i, j) for i in range(nq) for j in range(i + 1)]
    stacked = {}

    def put_scores(t):
        i, j = visits[t]
        if i not in stacked:
            stacked.clear()
            stacked[i] = stacked_q(i)
        s = lax.dot_general(stacked[i], k_ref[block_rows(j), :], nt, preferred_element_type=F32)
        s_ref[t % 2] = jnp.where(on_or_below_diag, s, NEG) if j == i else s

    def stage(t):
        i, j = visits[t]
        if t + 1 < len(visits):
            put_scores(t + 1)
        if j == 0:
            m_ref[...] = jnp.full_like(m_ref, NEG)
            acc_ref[...] = jnp.zeros_like(acc_ref)
        accumulate(s_ref[t % 2], j)
        if j == i:
            acc = acc_ref[...]
            o = acc[:tq, :dv] / acc[:tq, dv:] - lam * (acc[tq:, :dv] / acc[tq:, dv:])
            on = o * lax.rsqrt(jnp.mean(o * o, axis=-1, keepdims=True) + EPS) * out_gain
            o_ref[block_rows(i), :] = on.astype(BF16)

    put_scores(0)
    once = jnp.minimum(pl.program_id(0) + 1, 1)
    for t0 in range(0, len(visits), ATTN_REGION):
        def region(_, carry, t0=t0):
            stacked.clear()
            for t in range(t0, min(t0 + ATTN_REGION, len(visits))):
                stage(t)
            return carry

        lax.fori_loop(0, once, region, 0)
        stacked.clear()


def _attn(lam_p, qn, kn, dv, gain, batch, seq):
    t = qn.shape[0]
    tq = TQ_ATTN
    head = lambda b, h: (b, h)
    return pl.pallas_call(
        _attn_kernel,
        grid=(batch, DA_HEADS),
        in_specs=[pl.BlockSpec((4, DA_HEAD), lambda b, h: (0, 0)),
                  pl.BlockSpec((seq, DA_VDIM), head),
                  pl.BlockSpec((seq, DA_VDIM), head),
                  pl.BlockSpec((seq, DA_VDIM), head),
                  pl.BlockSpec((1, DA_VDIM), lambda b, h: (0, 0))],
        out_specs=pl.BlockSpec((seq, DA_VDIM), head),
        out_shape=jax.ShapeDtypeStruct((t, DA_WIDTH), BF16),
        scratch_shapes=[pltpu.VMEM((2, 2 * tq, tq), F32), pltpu.VMEM((2 * tq, LANES), F32),
                        pltpu.VMEM((2 * tq, 2 * DA_VDIM), F32)],
        compiler_params=_cparams(("parallel", "parallel")),
        name="attn",
    )(lam_p, qn, kn, dv, gain)


def _merge_kernel(ohg_ref, oda_ref, sgh_ref, sgd_ref, x_ref, wbh_ref, wbd_ref, wo_ref, nm_ref, wr_ref, br_ref,
                  xmid_ref, xn_ref, meta_ref, metat_ref, ttab_ref, cnt_ref, carry_ref):
    tm = x_ref.shape[0]

    @pl.when(pl.program_id(0) == 0)
    def _():
        carry_ref[...] = jnp.zeros_like(carry_ref)

    y_hg = jnp.dot(ohg_ref[...], wbh_ref[...], preferred_element_type=F32)
    y_da = jnp.dot(oda_ref[...], wbd_ref[...], preferred_element_type=F32)
    mixed = sgh_ref[...].astype(F32) * y_hg + sgd_ref[...].astype(F32) * y_da
    xm = x_ref[...] + jnp.dot(mixed.astype(BF16), wo_ref[...], preferred_element_type=F32)
    xmid_ref[...] = xm
    xn = xm * lax.rsqrt(jnp.mean(xm * xm, axis=-1, keepdims=True) + EPS) * nm_ref[...]
    xn_ref[...] = xn.astype(BF16)

    logit = jnp.dot(xn.astype(BF16), wr_ref[...].astype(BF16), preferred_element_type=F32) + br_ref[...]

    lane = lax.broadcasted_iota(jnp.int32, (tm, LANES), 1)
    lanef = lane.astype(F32)
    big = float(LANES)
    is_g = lane < N_GROUPS
    gl = jnp.where(is_g, logit, NEG)
    gmax = jnp.max(gl, axis=1, keepdims=True)
    g_sel = jnp.min(jnp.where(gl == gmax, lanef, big), axis=1, keepdims=True)
    g_w = 1.0 / jnp.sum(jnp.where(is_g, jnp.exp(gl - gmax), 0.0), axis=1, keepdims=True)
    lo = N_GROUPS + EXPERTS_PER_GROUP * g_sel
    in_grp = (lanef >= lo) & (lanef < lo + EXPERTS_PER_GROUP)
    el = jnp.where(in_grp, logit, NEG)
    v1 = jnp.max(el, axis=1, keepdims=True)
    i1 = jnp.min(jnp.where(el == v1, lanef, big), axis=1, keepdims=True)
    el2 = jnp.where(lanef == i1, NEG, el)
    v2 = jnp.max(el2, axis=1, keepdims=True)
    i2 = jnp.min(jnp.where(el2 == v2, lanef, big), axis=1, keepdims=True)
    e1 = i1 - N_GROUPS
    e2 = i2 - N_GROUPS
    d = jnp.exp(v2 - v1)
    w1 = g_w / (1.0 + d)
    w2 = g_w * d / (1.0 + d)

    oh1 = jnp.where(lanef == e1, 1.0, 0.0)
    oh2 = jnp.where(lanef == e2, 1.0, 0.0)
    r = lax.broadcasted_iota(jnp.int32, (tm, tm), 0)
    c = lax.broadcasted_iota(jnp.int32, (tm, tm), 1)
    below = jnp.where(c < r, 1.0, 0.0).astype(BF16)
    c1 = jnp.dot(below, oh1.astype(BF16), preferred_element_type=F32)
    c2 = jnp.dot(below, oh2.astype(BF16), preferred_element_type=F32)
    tot1 = jnp.sum(oh1, axis=0, keepdims=True)
    tot2 = jnp.sum(oh2, axis=0, keepdims=True)
    n_chunk = jnp.floor((tot1 + tot2 + (MOE_CHUNK - 1)) * (1.0 / MOE_CHUNK))
    r = lax.broadcasted_iota(jnp.int32, (LANES, LANES), 0)
    c = lax.broadcasted_iota(jnp.int32, (LANES, LANES), 1)
    before = jnp.where(r < c, 1.0, 0.0).astype(BF16)
    lstart = MOE_CHUNK * jnp.dot(jnp.broadcast_to(n_chunk, (SUBLANES, LANES)).astype(BF16), before,
                                 preferred_element_type=F32)[0:1]
    lo1 = jnp.sum(oh1 * (c1 + lstart), axis=1, keepdims=True)
    lo2 = jnp.sum(oh2 * (c2 + lstart + tot1), axis=1, keepdims=True)
    carry = carry_ref[...]
    ttab_ref[...] = jnp.concatenate([carry, MOE_CHUNK * n_chunk, lstart, jnp.zeros((SUBLANES - 3, LANES), F32)], axis=0)
    carry = carry + MOE_CHUNK * n_chunk
    carry_ref[...] = carry
    cnt_ref[...] = jnp.broadcast_to(carry, cnt_ref.shape)

    meta = jnp.where(lane == 0, e1, jnp.where(lane == 1, e2, jnp.where(lane == 2, lo1, jnp.where(
        lane == 3, lo2, jnp.where(lane == 4, w1, jnp.where(lane == 5, w2, 0.0))))))
    meta_ref[...] = meta
    metat_ref[...] = meta.T[0:SUBLANES, :]


def _merge(o_hg, o_da, sgh, sgd, x2, wbh, wbd, wo, norm_moe, wr, br):
    t = x2.shape[0]
    tm = TM_MOE
    nt = t // tm
    row = lambda w: pl.BlockSpec((tm, w), lambda i: (i, 0))
    const = lambda shape: pl.BlockSpec(shape, lambda i: (0, 0))
    return pl.pallas_call(
        _merge_kernel,
        grid=(nt,),
        in_specs=[row(HG_WIDTH), row(DA_WIDTH), row(D_MODEL), row(D_MODEL), row(D_MODEL),
                  const((HG_WIDTH, D_MODEL)), const((DA_WIDTH, D_MODEL)), const((D_MODEL, D_MODEL)),
                  const((1, D_MODEL)), const((D_MODEL, LANES)), const((1, LANES))],
        out_specs=[row(D_MODEL), row(D_MODEL), row(LANES),
                   pl.BlockSpec((SUBLANES, tm), lambda i: (0, i)),
                   pl.BlockSpec((SUBLANES, LANES), lambda i: (i, 0)),
                   const((SUBLANES, LANES))],
        out_shape=[jax.ShapeDtypeStruct((t, D_MODEL), F32),
                   jax.ShapeDtypeStruct((t, D_MODEL), BF16),
                   jax.ShapeDtypeStruct((t, LANES), F32),
                   jax.ShapeDtypeStruct((SUBLANES, t), F32),
                   jax.ShapeDtypeStruct((nt * SUBLANES, LANES), F32),
                   jax.ShapeDtypeStruct((SUBLANES, LANES), F32)],
        scratch_shapes=[pltpu.VMEM((1, LANES), F32)],
        compiler_params=_cparams(("arbitrary",)),
        name="merge",
    )(o_hg, o_da, sgh, sgd, x2, wbh, wbd, wo, norm_moe, wr, br)


def _for_each_chunk(tab_ref, tile, fn, group=MOE_DMA_GROUP):
    def one(c):
        fn(pl.multiple_of(c * MOE_CHUNK, MOE_CHUNK), pl.multiple_of(tab_ref[tile, c], MOE_CHUNK))

    def per_group(g, carry):
        for u in range(group):
            one(g * group + u)
        return carry

    def per_chunk(c, carry):
        one(c)
        return carry

    n = tab_ref[tile, TAB_COUNT]
    n_grouped = lax.div(n, group) * group
    lax.fori_loop(0, lax.div(n, group), per_group, 0)
    lax.fori_loop(n_grouped, n, per_chunk, 0)


def _dispatch_kernel(tab_ref, tail_ref, metat_ref, xn_ref, xg_hbm, xp_ref, zero_ref, sem, sem_z):
    i = pl.program_id(0)
    last = pl.num_programs(0) - 1
    lo1 = metat_ref[2:3, :]
    lo2 = metat_ref[3:4, :]
    r = lax.broadcasted_iota(jnp.int32, (MOE_LOCAL, xn_ref.shape[0]), 0).astype(F32)
    perm = jnp.where((r == lo1) | (r == lo2), 1.0, 0.0).astype(BF16)
    xp_ref[i % 2] = jnp.dot(perm, xn_ref[...], preferred_element_type=F32).astype(BF16)

    def chunk_copy(tile, lrow, grow):
        return pltpu.make_async_copy(xp_ref.at[tile % 2, pl.ds(lrow, MOE_CHUNK), :],
                                     xg_hbm.at[pl.ds(grow, MOE_CHUNK), :], sem.at[tile % 2])

    _for_each_chunk(tab_ref, i, lambda lrow, grow: chunk_copy(i, lrow, grow).start())

    @pl.when(i > 0)
    def _():
        _for_each_chunk(tab_ref, i - 1, lambda lrow, grow: chunk_copy(i - 1, lrow, grow).wait())

    def tail_copy(e, c):
        grow = pl.multiple_of(tail_ref[e] + c * MOE_CHUNK, MOE_CHUNK)
        return pltpu.make_async_copy(zero_ref.at[pl.ds(0, MOE_CHUNK), :], xg_hbm.at[pl.ds(grow, MOE_CHUNK), :], sem_z)

    def for_each_tail(fn):
        def per_expert(e, carry):
            lax.fori_loop(0, tail_ref[N_EXPERTS + e], lambda c, c2: (fn(e, c), c2)[1], 0)
            return carry
        lax.fori_loop(0, N_EXPERTS, per_expert, 0)

    def block_copy(b):
        grow = pl.multiple_of(b * TM_EXPERT, TM_EXPERT)
        return pltpu.make_async_copy(zero_ref, xg_hbm.at[pl.ds(grow, TM_EXPERT), :], sem_z)

    def for_each_free_block(fn):
        lax.fori_loop(tail_ref[2 * N_EXPERTS], xg_hbm.shape[0] // TM_EXPERT, lambda b, c2: (fn(b), c2)[1], 0)

    @pl.when(i == last)
    def _():
        zero_ref[...] = jnp.zeros_like(zero_ref)
        for_each_tail(lambda e, c: tail_copy(e, c).start())
        for_each_free_block(lambda b: block_copy(b).start())
        for_each_tail(lambda e, c: tail_copy(e, c).wait())
        for_each_free_block(lambda b: block_copy(b).wait())
        _for_each_chunk(tab_ref, i, lambda lrow, grow: chunk_copy(i, lrow, grow).wait())


def _dispatch(tab, tail, metat, xn, n_rows):
    t = xn.shape[0]
    tm = TM_MOE
    grid_spec = pltpu.PrefetchScalarGridSpec(
        num_scalar_prefetch=2,
        grid=(t // tm,),
        in_specs=[pl.BlockSpec((SUBLANES, tm), lambda i, tab, tail: (0, i)),
                  pl.BlockSpec((tm, D_MODEL), lambda i, tab, tail: (i, 0))],
        out_specs=pl.BlockSpec(memory_space=pl.ANY),
        scratch_shapes=[pltpu.VMEM((2, MOE_LOCAL, D_MODEL), BF16), pltpu.VMEM((TM_EXPERT, D_MODEL), BF16),
                        pltpu.SemaphoreType.DMA((2,)), pltpu.SemaphoreType.DMA],
    )
    return pl.pallas_call(
        _dispatch_kernel,
        grid_spec=grid_spec,
        out_shape=jax.ShapeDtypeStruct((n_rows, D_MODEL), BF16),
        compiler_params=_cparams(("arbitrary",)),
        name="dispatch",
    )(tab, tail, metat, xn)


def _expert_kernel(be_ref, nact_ref, xg_ref, w1_ref, w3_ref, w2_ref, y_ref, w1b_ref, w3b_ref, w2b_ref):
    i = pl.program_id(0)
    active = i < nact_ref[0]

    @pl.when(active & ((i == 0) | (be_ref[i] != be_ref[jnp.maximum(i - 1, 0)])))
    def _():
        w1b_ref[...] = w1_ref[0].astype(BF16)
        w3b_ref[...] = w3_ref[0].astype(BF16)
        w2b_ref[...] = w2_ref[0].astype(BF16)

    @pl.when(active)
    def _():
        xb = xg_ref[...]
        a = jnp.dot(xb, w1b_ref[...], preferred_element_type=F32)
        b = jnp.dot(xb, w3b_ref[...], preferred_element_type=F32)
        hid = (a * _sigmoid(a) * b).astype(BF16)
        y_ref[...] = jnp.dot(hid, w2b_ref[...], preferred_element_type=F32).astype(BF16)

    @pl.when(jnp.logical_not(active))
    def _():
        y_ref[...] = jnp.zeros_like(y_ref)


def _experts(block_expert, n_active, xg, w1, w3, w2):
    tm = TM_EXPERT
    nb = xg.shape[0] // tm
    blk = lambda i, be, na: jnp.minimum(i, na[0] - 1)
    grid_spec = pltpu.PrefetchScalarGridSpec(
        num_scalar_prefetch=2,
        grid=(nb,),
        in_specs=[pl.BlockSpec((tm, D_MODEL), lambda i, be, na: (blk(i, be, na), 0)),
                  pl.BlockSpec((1, D_MODEL, D_FF), lambda i, be, na: (be[blk(i, be, na)], 0, 0)),
                  pl.BlockSpec((1, D_MODEL, D_FF), lambda i, be, na: (be[blk(i, be, na)], 0, 0)),
                  pl.BlockSpec((1, D_FF, D_MODEL), lambda i, be, na: (be[blk(i, be, na)], 0, 0))],
        out_specs=pl.BlockSpec((tm, D_MODEL), lambda i, be, na: (i, 0)),
        scratch_shapes=[pltpu.VMEM((D_MODEL, D_FF), BF16), pltpu.VMEM((D_MODEL, D_FF), BF16),
                        pltpu.VMEM((D_FF, D_MODEL), BF16)],
    )
    return pl.pallas_call(
        _expert_kernel,
        grid_spec=grid_spec,
        out_shape=jax.ShapeDtypeStruct(xg.shape, BF16),
        compiler_params=_cparams(("arbitrary",)),
        name="experts",
    )(block_expert, n_active, xg, w1, w3, w2)


def _combine_kernel(tab_ref, meta_ref, xmid_ref, y_hbm, o_ref, yl_ref, sem):
    tm = xmid_ref.shape[0]
    i = pl.program_id(0)

    def chunk_copy(tile, lrow, grow):
        return pltpu.make_async_copy(y_hbm.at[pl.ds(grow, MOE_CHUNK), :],
                                     yl_ref.at[tile % 2, pl.ds(lrow, MOE_CHUNK), :], sem.at[tile % 2])

    @pl.when(i == 0)
    def _():
        yl_ref[...] = jnp.zeros_like(yl_ref)
        _for_each_chunk(tab_ref, i, lambda lrow, grow: chunk_copy(i, lrow, grow).start())

    @pl.when(i + 1 < pl.num_programs(0))
    def _():
        _for_each_chunk(tab_ref, i + 1, lambda lrow, grow: chunk_copy(i + 1, lrow, grow).start())

    meta = meta_ref[...]
    lane = lax.broadcasted_iota(jnp.int32, (tm, MOE_LOCAL), 1).astype(F32)
    wsel = (jnp.where(lane == meta[:, 2:3], meta[:, 4:5], 0.0)
            + jnp.where(lane == meta[:, 3:4], meta[:, 5:6], 0.0)).astype(BF16)

    _for_each_chunk(tab_ref, i, lambda lrow, grow: chunk_copy(i, lrow, grow).wait())
    o_ref[...] = xmid_ref[...] + jnp.dot(wsel, yl_ref[i % 2], preferred_element_type=F32)


def _combine(tab, meta, xmid, y):
    t = xmid.shape[0]
    tm = TM_MOE
    grid_spec = pltpu.PrefetchScalarGridSpec(
        num_scalar_prefetch=1,
        grid=(t // tm,),
        in_specs=[pl.BlockSpec((tm, LANES), lambda i, tab: (i, 0)),
                  pl.BlockSpec((tm, D_MODEL), lambda i, tab: (i, 0)),
                  pl.BlockSpec(memory_space=pl.ANY)],
        out_specs=pl.BlockSpec((tm, D_MODEL), lambda i, tab: (i, 0)),
        scratch_shapes=[pltpu.VMEM((2, MOE_LOCAL, D_MODEL), BF16), pltpu.SemaphoreType.DMA((2,))],
    )
    return pl.pallas_call(
        _combine_kernel,
        grid_spec=grid_spec,
        out_shape=jax.ShapeDtypeStruct((t, D_MODEL), F32),
        compiler_params=_cparams(("arbitrary",)),
        name="combine",
    )(tab, meta, xmid, y)


def kernel(x, norm_mix, w_in, hg_lb, hg_out_norm, da_q_norm, da_k_norm, da_lambda, da_out_norm, w_branch_hg,
           w_branch_da, w_out, norm_moe, w_router_group, b_router_group, w_router_expert, b_router_expert,
           w1, w3, w2):
    batch, seq, d = x.shape
    assert d == D_MODEL and norm_mix.shape[0] == 1 and w_in.shape[2] == IN_COLS
    assert seq % TM_PROJ == 0 and seq % TQ_ATTN == 0 and (batch * seq) % TM_MOE == 0
    t = batch * seq
    x2 = x.reshape(t, d)

    half = DA_HEAD // 2
    inv = ROPE_THETA ** (-jnp.arange(half, dtype=F32) / half)
    ang = jnp.arange(seq, dtype=F32)[:, None] * inv[None, :]
    cos_t = jnp.tile(jnp.cos(ang), (1, 2 * LANES // DA_HEAD))
    sin_t = jnp.tile(jnp.concatenate([-jnp.sin(ang), jnp.sin(ang)], axis=1), (1, LANES // DA_HEAD))

    reps = DA_WIDTH // DA_HEAD
    gq = jnp.tile(da_q_norm[0].astype(F32) * (DA_HEAD ** -0.5 * math.log2(math.e)), reps)[None, :]
    gk = jnp.tile(da_k_norm[0].astype(F32), reps)[None, :]

    o_hg, qn, kn, dv, sgh, sgd = _in_proj(
        x2, norm_mix.astype(F32), w_in[0].astype(BF16), hg_lb.astype(F32), gq, gk, cos_t, sin_t,
        hg_out_norm.astype(F32), seq)
    o_da = _attn(da_lambda[0].astype(F32), qn, kn, dv, da_out_norm.astype(F32), batch, seq)

    wr = jnp.zeros((D_MODEL, LANES), F32)
    wr = wr.at[:, :N_GROUPS].set(w_router_group[0]).at[:, N_GROUPS:N_GROUPS + N_EXPERTS].set(w_router_expert[0])
    br = jnp.zeros((1, LANES), F32)
    br = br.at[0, :N_GROUPS].set(b_router_group[0]).at[0, N_GROUPS:N_GROUPS + N_EXPERTS].set(b_router_expert[0])

    xmid, xn, meta, metat, ttab, cnt = _merge(
        o_hg, o_da, sgh, sgd, x2, w_branch_hg[0].astype(BF16), w_branch_da[0].astype(BF16), w_out[0].astype(BF16),
        norm_moe.astype(F32), wr, br)

    tmx = TM_EXPERT
    nt = t // TM_MOE
    ttab = ttab.reshape(nt, SUBLANES, LANES)[:, :, :N_EXPERTS].astype(jnp.int32)
    counts = cnt[0, :N_EXPERTS].astype(jnp.int32)
    padded = (counts + tmx - 1) // tmx * tmx
    pad_end = jnp.cumsum(padded)
    pad_start = pad_end - padded
    gstart, seg_len, lstart = pad_start[None, :] + ttab[:, 0], ttab[:, 1], ttab[:, 2]
    lrow = (jnp.arange(TAB_COUNT, dtype=jnp.int32) * MOE_CHUNK)[None, :, None]
    in_seg = (lrow >= lstart[:, None, :]) & (lrow < (lstart + seg_len)[:, None, :])
    chunk_grow = jnp.sum(jnp.where(in_seg, gstart[:, None, :] + lrow - lstart[:, None, :], 0), axis=2)
    tab = jnp.concatenate([chunk_grow, jnp.sum(seg_len, axis=1, keepdims=True) // MOE_CHUNK,
                           jnp.zeros((nt, LANES - TAB_COUNT - 1), jnp.int32)], axis=1)
    tail = jnp.concatenate([pad_start + counts, (padded - counts) // MOE_CHUNK, pad_end[-1:] // tmx])
    n_rows = t * TOP_K + nt * N_EXPERTS * MOE_CHUNK + N_EXPERTS * tmx
    nb = n_rows // tmx
    block_start = jnp.arange(nb, dtype=jnp.int32) * tmx
    block_expert = jnp.minimum(jnp.sum(pad_end[None, :] <= block_start[:, None], axis=1),
                               N_EXPERTS - 1).astype(jnp.int32)
    n_active = (pad_end[-1:] // tmx).astype(jnp.int32)

    xg = _dispatch(tab, tail, metat, xn, n_rows)
    y = _experts(block_expert, n_active, xg, w1[0], w3[0], w2[0])
    out = _combine(tab, meta, xmid, y)
    return out.reshape(batch, seq, d)
```

```python
import functools
import math

import jax
import jax.numpy as jnp
from jax import lax
from jax.experimental import pallas as pl
from jax.experimental.pallas import tpu as pltpu

F32 = jnp.float32
BF16 = jnp.bfloat16

D_MODEL = 1024
HG_HEADS = 4
HG_DK = 128
HG_WIDTH = HG_HEADS * HG_DK
HG_CHUNK = 64
DA_HEADS = 4
DA_HEAD = 64
DA_VDIM = 2 * DA_HEAD
DA_WIDTH = DA_HEADS * DA_VDIM
ROPE_THETA = 10000.0
N_GROUPS = 4
EXPERTS_PER_GROUP = 8
N_EXPERTS = N_GROUPS * EXPERTS_PER_GROUP
TOP_K = 2
D_FF = 512
EPS = 1e-6
LAM_INIT = 0.8 - 0.6 * math.exp(-0.3 * 0)
IN_COLS = 4 * HG_WIDTH + 3 * DA_WIDTH + 2 * D_MODEL

LANES = 128
SUBLANES = 8
BF16_ROWS = 16
NEG = -1e30

TM_PROJ = 512
HGRN_GROUP = 256
TQ_ATTN = 512
ATTN_REGION = 9
TM_MOE = 512
TM_EXPERT = 512
MOE_CHUNK = BF16_ROWS
MOE_LOCAL = TOP_K * TM_MOE + N_EXPERTS * MOE_CHUNK
TAB_COUNT = MOE_LOCAL // MOE_CHUNK
MOE_DMA_GROUP = 8
MOE_KBLOCK = 256
VMEM_LIMIT = 56 * 1024 * 1024


def _cparams(sem):
    return pltpu.CompilerParams(dimension_semantics=sem, vmem_limit_bytes=VMEM_LIMIT)


def _sigmoid(v):
    return 1.0 / (1.0 + jnp.exp(-v))


def _qk_prep(d, gain, cos, sin, gmat):
    ss = jnp.dot((d * d).astype(BF16), gmat, preferred_element_type=F32)
    y = d * lax.rsqrt(ss * (1.0 / DA_HEAD) + EPS) * gain
    lane = lax.broadcasted_iota(jnp.int32, (d.shape[0], LANES), 1)
    upper = (lane & (DA_HEAD // 2)) != 0
    outs = []
    for c in range(d.shape[1] // LANES):
        yc = y[:, c * LANES:(c + 1) * LANES]
        sw = jnp.where(upper, pltpu.roll(yc, DA_HEAD // 2, 1), pltpu.roll(yc, LANES - DA_HEAD // 2, 1))
        outs.append(yc * cos + sw * sin)
    return jnp.concatenate(outs, axis=1)


def _in_proj_kernel(x_ref, nm_ref, w_ref, lb_ref, gq_ref, gk_ref, cos_ref, sin_ref, hgain_ref,
                    ohg_ref, qn_ref, kn_ref, dv_ref, sgh_ref, sgd_ref, st_ref, *, tiles_per_seq):
    @pl.when(pl.program_id(0) % tiles_per_seq == 0)
    def _():
        st_ref[...] = jnp.zeros_like(st_ref)

    x = x_ref[...]
    h = x * lax.rsqrt(jnp.mean(x * x, axis=-1, keepdims=True) + EPS) * nm_ref[...]
    hb = h.astype(BF16)

    def proj(c0, width):
        return jnp.dot(hb, w_ref[:, c0:c0 + width], preferred_element_type=F32)

    hq = proj(0, HG_WIDTH)
    hf = proj(HG_WIDTH, HG_WIDTH)
    lbp = lb_ref[...]
    mx = jnp.maximum(lbp[0:1], lbp[1:2])
    e0 = jnp.exp(lbp[0:1] - mx)
    e1 = jnp.exp(lbp[1:2] - mx)
    lb = e0 / (e0 + e1)
    f = lb + (1.0 - lb) * _sigmoid(hf)
    hi = proj(2 * HG_WIDTH, HG_WIDTH)
    hog = proj(3 * HG_WIDTH, HG_WIDTH)
    hgrn_head = _hgrn_tile(hq, jnp.log(f), 1.0 - f, hi, hog * _sigmoid(hog), hgain_ref[...], st_ref)

    def put_head(h):
        ohg_ref[:, h * HG_DK:(h + 1) * HG_DK] = hgrn_head(h)

    r = lax.broadcasted_iota(jnp.int32, (DA_WIDTH, DA_WIDTH), 0) // DA_HEAD
    c = lax.broadcasted_iota(jnp.int32, (DA_WIDTH, DA_WIDTH), 1) // DA_HEAD
    gmat = jnp.where(r == c, 1.0, 0.0).astype(BF16)
    cos = cos_ref[...]
    sin = sin_ref[...]
    base = 4 * HG_WIDTH
    qn_ref[...] = _qk_prep(proj(base, DA_WIDTH), gq_ref[...], cos, sin, gmat).astype(BF16)
    put_head(0)
    kn_ref[...] = _qk_prep(proj(base + DA_WIDTH, DA_WIDTH), gk_ref[...], cos, sin, gmat).astype(BF16)
    put_head(1)
    dv_ref[...] = proj(base + 2 * DA_WIDTH, DA_WIDTH).astype(BF16)
    base += 3 * DA_WIDTH
    sgh_ref[...] = _sigmoid(proj(base, D_MODEL)).astype(BF16)
    put_head(2)
    sgd_ref[...] = _sigmoid(proj(base + D_MODEL, D_MODEL)).astype(BF16)
    put_head(3)


def _in_proj(x2, norm_mix, w_in_bf, hg_lb, gq, gk, cos_t, sin_t, hg_gain, seq):
    t = x2.shape[0]
    tm = TM_PROJ
    nseq = seq // tm
    row = lambda w: pl.BlockSpec((tm, w), lambda i: (i, 0))
    const = lambda shape: pl.BlockSpec(shape, lambda i: (0, 0))
    tab = pl.BlockSpec((tm, LANES), lambda i: (i % nseq, 0))
    out_shape = [
        jax.ShapeDtypeStruct((t, HG_WIDTH), BF16),
        jax.ShapeDtypeStruct((t, DA_WIDTH), BF16),
        jax.ShapeDtypeStruct((t, DA_WIDTH), BF16),
        jax.ShapeDtypeStruct((t, DA_WIDTH), BF16),
        jax.ShapeDtypeStruct((t, D_MODEL), BF16),
        jax.ShapeDtypeStruct((t, D_MODEL), BF16),
    ]
    out_specs = [row(HG_WIDTH)] + [row(DA_WIDTH)] * 3 + [row(D_MODEL)] * 2
    return pl.pallas_call(
        functools.partial(_in_proj_kernel, tiles_per_seq=nseq),
        grid=(t // tm,),
        in_specs=[row(D_MODEL), const((1, D_MODEL)), const((D_MODEL, IN_COLS)), const((2, HG_WIDTH)),
                  const((1, DA_WIDTH)), const((1, DA_WIDTH)), tab, tab, const((1, HG_DK))],
        out_specs=out_specs,
        out_shape=out_shape,
        scratch_shapes=[pltpu.VMEM((HG_HEADS, HG_DK, HG_DK), F32)],
        compiler_params=_cparams(("arbitrary",)),
        name="in_proj",
    )(x2, norm_mix, w_in_bf, hg_lb, gq, gk, cos_t, sin_t, hg_gain)


def _hgrn_tile(q, g, k, v32, og, gain, st_ref):
    c = HG_CHUNK
    tc = q.shape[0]
    nc = tc // c
    grp = HGRN_GROUP
    row = lax.broadcasted_iota(jnp.int32, (grp, grp), 0)
    col = lax.broadcasted_iota(jnp.int32, (grp, grp), 1)
    causal = (row // c == col // c) & (col <= row)
    tri = jnp.where(causal, 1.0, 0.0).astype(BF16)
    nt = (((1,), (1,)), ((), ()))
    pair_lane = lax.broadcasted_iota(jnp.int32, (HG_DK, 2 * c), 1)
    in_chunk = [pair_lane < c, pair_lane >= c]

    g_hi = g.astype(BF16)
    g_lo = (g - g_hi.astype(F32)).astype(BF16)
    b = jnp.concatenate(
        [jnp.dot(tri, g_hi[r0:r0 + grp], preferred_element_type=F32)
         + jnp.dot(tri, g_lo[r0:r0 + grp], preferred_element_type=F32) for r0 in range(0, tc, grp)], axis=0)
    b_mid_rows = [b[ci * c + c // 2 - 1:ci * c + c // 2, :] for ci in range(nc)]
    b_end_rows = [b[ci * c + c - 1:ci * c + c, :] for ci in range(nc)]
    per_chunk = lambda rows: jnp.concatenate([jnp.broadcast_to(r, (c, r.shape[1])) for r in rows], axis=0)
    b_mid = per_chunk(b_mid_rows)
    e_fwd = jnp.exp(b - b_mid)
    e_bwd = jnp.exp(b_mid - b)
    q_rel32 = q * e_fwd
    k_rel32 = k * e_bwd
    q_rel = q_rel32.astype(BF16)
    k_rel = k_rel32.astype(BF16)
    q_st = (q_rel32 * per_chunk([jnp.exp(r) for r in b_mid_rows])).astype(BF16)
    k_end = (k_rel32 * per_chunk([jnp.exp(e - m) for e, m in zip(b_end_rows, b_mid_rows)])).astype(BF16)
    decay = [jnp.exp(r) for r in b_end_rows]
    v = v32.astype(BF16)

    def head(h):
        cols = slice(h * HG_DK, (h + 1) * HG_DK)
        intra = []
        for r0 in range(0, tc, grp):
            rows = slice(r0, r0 + grp)
            sc = lax.dot_general(q_rel[rows, cols], k_rel[rows, cols], nt, preferred_element_type=F32)
            sc = jnp.where(causal, sc, 0.0).astype(BF16)
            intra.append(jnp.dot(sc, v[rows, cols], preferred_element_type=F32))
        v_t = v32[:, cols].T.astype(BF16)
        incr = []
        for ci in range(nc):
            pair = slice((ci // 2) * 2 * c, (ci // 2 + 1) * 2 * c)
            lhs = jnp.where(in_chunk[ci % 2], v_t[:, pair], jnp.zeros((HG_DK, 2 * c), BF16))
            incr.append(jnp.dot(lhs, k_end[pair, cols], preferred_element_type=F32))
        st = st_ref[h]
        inter = []
        for ci in range(nc):
            rows = slice(ci * c, (ci + 1) * c)
            inter.append(lax.dot_general(q_st[rows, cols], st.astype(BF16), nt, preferred_element_type=F32))
            st = st * decay[ci][:, cols] + incr[ci]
        st_ref[h] = st
        o = jnp.concatenate(intra, axis=0) + jnp.concatenate(inter, axis=0)
        on = o * lax.rsqrt(jnp.mean(o * o, axis=-1, keepdims=True) + EPS) * gain
        return (on * og[:, cols]).astype(BF16)

    return head


def _attn_kernel(lam_ref, q_ref, k_ref, v_ref, gain_ref, o_ref, s_ref, m_ref, acc_ref):
    tq = TQ_ATTN
    dv = DA_VDIM
    nq = q_ref.shape[0] // tq
    nt = (((1,), (1,)), ((), ()))
    ones = jnp.ones((tq, dv), BF16)
    lane = lax.broadcasted_iota(jnp.int32, (tq, dv), 1)
    zero = jnp.zeros((tq, dv), BF16)
    r = lax.broadcasted_iota(jnp.int32, (2 * tq, tq), 0)
    c = lax.broadcasted_iota(jnp.int32, (2 * tq, tq), 1)
    on_or_below_diag = c <= jnp.where(r >= tq, r - tq, r)
    lp = lam_ref[...]
    lam = (jnp.exp(jnp.sum(lp[0:1] * lp[1:2], axis=1, keepdims=True))
           - jnp.exp(jnp.sum(lp[2:3] * lp[3:4], axis=1, keepdims=True)) + LAM_INIT)
    out_gain = gain_ref[...] * (1.0 - LAM_INIT)

    def block_rows(j):
        return slice(j * tq, (j + 1) * tq)

    def stacked_q(i):
        q = q_ref[block_rows(i), :]
        return jnp.concatenate([jnp.where(lane < DA_HEAD, q, zero), jnp.where(lane >= DA_HEAD, q, zero)], axis=0)

    def accumulate(s, j):
        v1 = jnp.concatenate([v_ref[block_rows(j), :], ones], axis=1)
        m_prev = m_ref[...]
        m_new = jnp.maximum(m_prev, jnp.max(s, axis=1, keepdims=True))
        alpha = jnp.exp2(m_prev - m_new)
        p = jnp.exp2(s - jnp.tile(m_new, (1, tq // LANES)))
        acc_ref[...] = (jnp.tile(alpha, (1, 2)) * acc_ref[...]
                        + jnp.dot(p.astype(BF16), v1, preferred_element_type=F32))
        m_ref[...] = m_new

    visits = [(i, j) for i in range(nq) for j in range(i + 1)]
    stacked = {}

    def put_scores(t):
        i, j = visits[t]
        if i not in stacked:
            stacked.clear()
            stacked[i] = stacked_q(i)
        s = lax.dot_general(stacked[i], k_ref[block_rows(j), :], nt, preferred_element_type=F32)
        s_ref[t % 2] = jnp.where(on_or_below_diag, s, NEG) if j == i else s

    def stage(t):
        i, j = visits[t]
        if t + 1 < len(visits):
            put_scores(t + 1)
        if j == 0:
            m_ref[...] = jnp.full_like(m_ref, NEG)
            acc_ref[...] = jnp.zeros_like(acc_ref)
        accumulate(s_ref[t % 2], j)
        if j == i:
            acc = acc_ref[...]
            o = acc[:tq, :dv] / acc[:tq, dv:] - lam * (acc[tq:, :dv] / acc[tq:, dv:])
            on = o * lax.rsqrt(jnp.mean(o * o, axis=-1, keepdims=True) + EPS) * out_gain
            o_ref[block_rows(i), :] = on.astype(BF16)

    put_scores(0)
    once = jnp.minimum(pl.program_id(0) + 1, 1)
    for t0 in range(0, len(visits), ATTN_REGION):
        def region(_, carry, t0=t0):
            stacked.clear()
            for t in range(t0, min(t0 + ATTN_REGION, len(visits))):
                stage(t)
            return carry

        lax.fori_loop(0, once, region, 0)
        stacked.clear()


def _attn(lam_p, qn, kn, dv, gain, batch, seq):
    t = qn.shape[0]
    tq = TQ_ATTN
    head = lambda b, h: (b, h)
    return pl.pallas_call(
        _attn_kernel,
        grid=(batch, DA_HEADS),
        in_specs=[pl.BlockSpec((4, DA_HEAD), lambda b, h: (0, 0)),
                  pl.BlockSpec((seq, DA_VDIM), head),
                  pl.BlockSpec((seq, DA_VDIM), head),
                  pl.BlockSpec((seq, DA_VDIM), head),
                  pl.BlockSpec((1, DA_VDIM), lambda b, h: (0, 0))],
        out_specs=pl.BlockSpec((seq, DA_VDIM), head),
        out_shape=jax.ShapeDtypeStruct((t, DA_WIDTH), BF16),
        scratch_shapes=[pltpu.VMEM((2, 2 * tq, tq), F32), pltpu.VMEM((2 * tq, LANES), F32),
                        pltpu.VMEM((2 * tq, 2 * DA_VDIM), F32)],
        compiler_params=_cparams(("parallel", "parallel")),
        name="attn",
    )(lam_p, qn, kn, dv, gain)


def _merge_kernel(ohg_ref, oda_ref, sgh_ref, sgd_ref, x_ref, wbh_ref, wbd_ref, wo_ref, nm_ref, wr_ref, br_ref,
                  xmid_ref, xn_ref, meta_ref, metat_ref, ttab_ref, cnt_ref, carry_ref, logit_ref):
    tm = x_ref.shape[0]
    i = pl.program_id(0)

    @pl.when(i == 0)
    def _():
        carry_ref[...] = jnp.zeros_like(carry_ref)
        logit_ref[...] = jnp.zeros_like(logit_ref)

    logit = logit_ref[...]

    y_hg = jnp.dot(ohg_ref[...], wbh_ref[...], preferred_element_type=F32)
    y_da = jnp.dot(oda_ref[...], wbd_ref[...], preferred_element_type=F32)
    mixed = sgh_ref[...].astype(F32) * y_hg + sgd_ref[...].astype(F32) * y_da
    xm = x_ref[...] + jnp.dot(mixed.astype(BF16), wo_ref[...], preferred_element_type=F32)
    xmid_ref[...] = xm
    xn = xm * lax.rsqrt(jnp.mean(xm * xm, axis=-1, keepdims=True) + EPS) * nm_ref[...]
    xn_ref[...] = xn.astype(BF16)
    logit_ref[...] = jnp.dot(xn.astype(BF16), wr_ref[...].astype(BF16), preferred_element_type=F32) + br_ref[...]

    lane = lax.broadcasted_iota(jnp.int32, (tm, LANES), 1)
    lanef = lane.astype(F32)
    big = float(LANES)
    is_g = lane < N_GROUPS
    gl = jnp.where(is_g, logit, NEG)
    gmax = jnp.max(gl, axis=1, keepdims=True)
    g_sel = jnp.min(jnp.where(gl == gmax, lanef, big), axis=1, keepdims=True)
    g_w = 1.0 / jnp.sum(jnp.where(is_g, jnp.exp(gl - gmax), 0.0), axis=1, keepdims=True)
    lo = N_GROUPS + EXPERTS_PER_GROUP * g_sel
    in_grp = (lanef >= lo) & (lanef < lo + EXPERTS_PER_GROUP)
    el = jnp.where(in_grp, logit, NEG)
    v1 = jnp.max(el, axis=1, keepdims=True)
    i1 = jnp.min(jnp.where(el == v1, lanef, big), axis=1, keepdims=True)
    el2 = jnp.where(lanef == i1, NEG, el)
    v2 = jnp.max(el2, axis=1, keepdims=True)
    i2 = jnp.min(jnp.where(el2 == v2, lanef, big), axis=1, keepdims=True)
    e1 = i1 - N_GROUPS
    e2 = i2 - N_GROUPS
    d = jnp.exp(v2 - v1)
    w1 = g_w / (1.0 + d)
    w2 = g_w * d / (1.0 + d)

    oh1 = jnp.where(lanef == e1, 1.0, 0.0)
    oh2 = jnp.where(lanef == e2, 1.0, 0.0)
    r = lax.broadcasted_iota(jnp.int32, (tm, tm), 0)
    c = lax.broadcasted_iota(jnp.int32, (tm, tm), 1)
    below = jnp.where(c < r, 1.0, 0.0).astype(BF16)
    c1 = jnp.dot(below, oh1.astype(BF16), preferred_element_type=F32)
    c2 = jnp.dot(below, oh2.astype(BF16), preferred_element_type=F32)
    tot1 = jnp.sum(oh1, axis=0, keepdims=True)
    tot2 = jnp.sum(oh2, axis=0, keepdims=True)
    n_chunk = jnp.floor((tot1 + tot2 + (MOE_CHUNK - 1)) * (1.0 / MOE_CHUNK))
    r = lax.broadcasted_iota(jnp.int32, (LANES, LANES), 0)
    c = lax.broadcasted_iota(jnp.int32, (LANES, LANES), 1)
    before = jnp.where(r < c, 1.0, 0.0).astype(BF16)
    lstart = MOE_CHUNK * jnp.dot(jnp.broadcast_to(n_chunk, (SUBLANES, LANES)).astype(BF16), before,
                                 preferred_element_type=F32)[0:1]
    lo1 = jnp.sum(oh1 * (c1 + lstart), axis=1, keepdims=True)
    lo2 = jnp.sum(oh2 * (c2 + lstart + tot1), axis=1, keepdims=True)
    carry = carry_ref[...]
    ttab_ref[...] = jnp.concatenate([carry, MOE_CHUNK * n_chunk, lstart, jnp.zeros((SUBLANES - 3, LANES), F32)], axis=0)
    carry = carry + jnp.where(i >= 1, MOE_CHUNK * n_chunk, 0.0)
    carry_ref[...] = carry
    cnt_ref[...] = jnp.broadcast_to(carry, cnt_ref.shape)

    meta = jnp.where(lane == 0, e1, jnp.where(lane == 1, e2, jnp.where(lane == 2, lo1, jnp.where(
        lane == 3, lo2, jnp.where(lane == 4, w1, jnp.where(lane == 5, w2, 0.0))))))
    meta_ref[...] = meta
    metat_ref[...] = meta.T[0:SUBLANES, :]


def _merge(o_hg, o_da, sgh, sgd, x2, wbh, wbd, wo, norm_moe, wr, br):
    t = x2.shape[0]
    tm = TM_MOE
    nt = t // tm
    cur = lambda i: jnp.minimum(i, nt - 1)
    prev = lambda i: jnp.maximum(i - 1, 0)
    row = lambda w: pl.BlockSpec((tm, w), lambda i: (cur(i), 0))
    const = lambda shape: pl.BlockSpec(shape, lambda i: (0, 0))
    return pl.pallas_call(
        _merge_kernel,
        grid=(nt + 1,),
        in_specs=[row(HG_WIDTH), row(DA_WIDTH), row(D_MODEL), row(D_MODEL), row(D_MODEL),
                  const((HG_WIDTH, D_MODEL)), const((DA_WIDTH, D_MODEL)), const((D_MODEL, D_MODEL)),
                  const((1, D_MODEL)), const((D_MODEL, LANES)), const((1, LANES))],
        out_specs=[row(D_MODEL), row(D_MODEL),
                   pl.BlockSpec((tm, LANES), lambda i: (prev(i), 0)),
                   pl.BlockSpec((SUBLANES, tm), lambda i: (0, prev(i))),
                   pl.BlockSpec((SUBLANES, LANES), lambda i: (prev(i), 0)),
                   const((SUBLANES, LANES))],
        out_shape=[jax.ShapeDtypeStruct((t, D_MODEL), F32),
                   jax.ShapeDtypeStruct((t, D_MODEL), BF16),
                   jax.ShapeDtypeStruct((t, LANES), F32),
                   jax.ShapeDtypeStruct((SUBLANES, t), F32),
                   jax.ShapeDtypeStruct((nt * SUBLANES, LANES), F32),
                   jax.ShapeDtypeStruct((SUBLANES, LANES), F32)],
        scratch_shapes=[pltpu.VMEM((1, LANES), F32), pltpu.VMEM((tm, LANES), F32)],
        compiler_params=_cparams(("arbitrary",)),
        name="merge",
    )(o_hg, o_da, sgh, sgd, x2, wbh, wbd, wo, norm_moe, wr, br)


def _for_each_chunk(tab_ref, tile, fn, group=MOE_DMA_GROUP):
    def one(c):
        fn(pl.multiple_of(c * MOE_CHUNK, MOE_CHUNK), pl.multiple_of(tab_ref[tile, c], MOE_CHUNK))

    def per_group(g, carry):
        for u in range(group):
            one(g * group + u)
        return carry

    def per_chunk(c, carry):
        one(c)
        return carry

    n = tab_ref[tile, TAB_COUNT]
    n_grouped = lax.div(n, group) * group
    lax.fori_loop(0, lax.div(n, group), per_group, 0)
    lax.fori_loop(n_grouped, n, per_chunk, 0)


def _dispatch_kernel(tab_ref, tail_ref, metat_ref, xn_ref, xg_hbm, xp_ref, zero_ref, sem, sem_z):
    i = pl.program_id(0)
    last = pl.num_programs(0) - 1
    lo1 = metat_ref[2:3, :]
    lo2 = metat_ref[3:4, :]
    xn = xn_ref[...]
    for r0 in range(0, MOE_LOCAL, MOE_KBLOCK):
        r = (lax.broadcasted_iota(jnp.int32, (MOE_KBLOCK, xn.shape[0]), 0) + r0).astype(F32)
        perm = jnp.where((r == lo1) | (r == lo2), 1.0, 0.0).astype(BF16)
        xp_ref[i % 2, r0:r0 + MOE_KBLOCK, :] = jnp.dot(perm, xn, preferred_element_type=F32).astype(BF16)

    def chunk_copy(tile, lrow, grow):
        return pltpu.make_async_copy(xp_ref.at[tile % 2, pl.ds(lrow, MOE_CHUNK), :],
                                     xg_hbm.at[pl.ds(grow, MOE_CHUNK), :], sem.at[tile % 2])

    _for_each_chunk(tab_ref, i, lambda lrow, grow: chunk_copy(i, lrow, grow).start())

    @pl.when(i > 0)
    def _():
        _for_each_chunk(tab_ref, i - 1, lambda lrow, grow: chunk_copy(i - 1, lrow, grow).wait())

    def tail_copy(e, c):
        grow = pl.multiple_of(tail_ref[e] + c * MOE_CHUNK, MOE_CHUNK)
        return pltpu.make_async_copy(zero_ref.at[pl.ds(0, MOE_CHUNK), :], xg_hbm.at[pl.ds(grow, MOE_CHUNK), :], sem_z)

    def for_each_tail(fn):
        def per_expert(e, carry):
            lax.fori_loop(0, tail_ref[N_EXPERTS + e], lambda c, c2: (fn(e, c), c2)[1], 0)
            return carry
        lax.fori_loop(0, N_EXPERTS, per_expert, 0)

    def block_copy(b):
        grow = pl.multiple_of(b * TM_EXPERT, TM_EXPERT)
        return pltpu.make_async_copy(zero_ref, xg_hbm.at[pl.ds(grow, TM_EXPERT), :], sem_z)

    def for_each_free_block(fn):
        lax.fori_loop(tail_ref[2 * N_EXPERTS], xg_hbm.shape[0] // TM_EXPERT, lambda b, c2: (fn(b), c2)[1], 0)

    @pl.when(i == last)
    def _():
        zero_ref[...] = jnp.zeros_like(zero_ref)
        for_each_tail(lambda e, c: tail_copy(e, c).start())
        for_each_free_block(lambda b: block_copy(b).start())
        for_each_tail(lambda e, c: tail_copy(e, c).wait())
        for_each_free_block(lambda b: block_copy(b).wait())
        _for_each_chunk(tab_ref, i, lambda lrow, grow: chunk_copy(i, lrow, grow).wait())


def _dispatch(tab, tail, metat, xn, n_rows):
    t = xn.shape[0]
    tm = TM_MOE
    grid_spec = pltpu.PrefetchScalarGridSpec(
        num_scalar_prefetch=2,
        grid=(t // tm,),
        in_specs=[pl.BlockSpec((SUBLANES, tm), lambda i, tab, tail: (0, i)),
                  pl.BlockSpec((tm, D_MODEL), lambda i, tab, tail: (i, 0))],
        out_specs=pl.BlockSpec(memory_space=pl.ANY),
        scratch_shapes=[pltpu.VMEM((2, MOE_LOCAL, D_MODEL), BF16), pltpu.VMEM((TM_EXPERT, D_MODEL), BF16),
                        pltpu.SemaphoreType.DMA((2,)), pltpu.SemaphoreType.DMA],
    )
    return pl.pallas_call(
        _dispatch_kernel,
        grid_spec=grid_spec,
        out_shape=jax.ShapeDtypeStruct((n_rows, D_MODEL), BF16),
        compiler_params=_cparams(("arbitrary",)),
        name="dispatch",
    )(tab, tail, metat, xn)


def _expert_kernel(be_ref, nact_ref, xg_ref, w1_ref, w3_ref, w2_ref, y_ref, w1b_ref, w3b_ref, w2b_ref):
    i = pl.program_id(0)
    active = i < nact_ref[0]

    @pl.when(active & ((i == 0) | (be_ref[i] != be_ref[jnp.maximum(i - 1, 0)])))
    def _():
        w1b_ref[...] = w1_ref[0].astype(BF16)
        w3b_ref[...] = w3_ref[0].astype(BF16)
        w2b_ref[...] = w2_ref[0].astype(BF16)

    @pl.when(active)
    def _():
        xb = xg_ref[...]
        a = jnp.dot(xb, w1b_ref[...], preferred_element_type=F32)
        b = jnp.dot(xb, w3b_ref[...], preferred_element_type=F32)
        hid = (a * _sigmoid(a) * b).astype(BF16)
        y_ref[...] = jnp.dot(hid, w2b_ref[...], preferred_element_type=F32).astype(BF16)

    @pl.when(jnp.logical_not(active))
    def _():
        y_ref[...] = jnp.zeros_like(y_ref)


def _experts(block_expert, n_active, xg, w1, w3, w2):
    tm = TM_EXPERT
    nb = xg.shape[0] // tm
    blk = lambda i, be, na: jnp.minimum(i, na[0] - 1)
    grid_spec = pltpu.PrefetchScalarGridSpec(
        num_scalar_prefetch=2,
        grid=(nb,),
        in_specs=[pl.BlockSpec((tm, D_MODEL), lambda i, be, na: (blk(i, be, na), 0)),
                  pl.BlockSpec((1, D_MODEL, D_FF), lambda i, be, na: (be[blk(i, be, na)], 0, 0)),
                  pl.BlockSpec((1, D_MODEL, D_FF), lambda i, be, na: (be[blk(i, be, na)], 0, 0)),
                  pl.BlockSpec((1, D_FF, D_MODEL), lambda i, be, na: (be[blk(i, be, na)], 0, 0))],
        out_specs=pl.BlockSpec((tm, D_MODEL), lambda i, be, na: (i, 0)),
        scratch_shapes=[pltpu.VMEM((D_MODEL, D_FF), BF16), pltpu.VMEM((D_MODEL, D_FF), BF16),
                        pltpu.VMEM((D_FF, D_MODEL), BF16)],
    )
    return pl.pallas_call(
        _expert_kernel,
        grid_spec=grid_spec,
        out_shape=jax.ShapeDtypeStruct(xg.shape, BF16),
        compiler_params=_cparams(("arbitrary",)),
        name="experts",
    )(block_expert, n_active, xg, w1, w3, w2)


def _combine_kernel(tab_ref, meta_ref, xmid_ref, y_hbm, o_ref, yl_ref, sem):
    tm = xmid_ref.shape[0]
    i = pl.program_id(0)
    last = pl.num_programs(0) - 1
    nxt = jnp.minimum(i + 1, last)

    def chunk_copy(tile, slot, c):
        grow = pl.multiple_of(tab_ref[tile, c], MOE_CHUNK)
        return pltpu.make_async_copy(y_hbm.at[pl.ds(grow, MOE_CHUNK), :],
                                     yl_ref.at[slot, pl.ds(c * MOE_CHUNK, MOE_CHUNK), :], sem.at[slot])

    @pl.when(i == 0)
    def _():
        for c in range(TAB_COUNT):
            chunk_copy(i, 0, c).start()

    for c in range(TAB_COUNT):
        chunk_copy(nxt, (i + 1) % 2, c).start()

    meta = meta_ref[...]
    kb = MOE_KBLOCK
    acc = xmid_ref[...]
    for k0 in range(0, MOE_LOCAL, kb):
        lane = (lax.broadcasted_iota(jnp.int32, (tm, kb), 1) + k0).astype(F32)
        wsel = (jnp.where(lane == meta[:, 2:3], meta[:, 4:5], 0.0)
                + jnp.where(lane == meta[:, 3:4], meta[:, 5:6], 0.0)).astype(BF16)
        if k0 == 0:
            for c in range(TAB_COUNT):
                chunk_copy(i, i % 2, c).wait()
        acc = acc + jnp.dot(wsel, yl_ref[i % 2, k0:k0 + kb, :], preferred_element_type=F32)
    o_ref[...] = acc

    @pl.when(i == last)
    def _():
        for c in range(TAB_COUNT):
            chunk_copy(nxt, (i + 1) % 2, c).wait()


def _combine(tab, meta, xmid, y):
    t = xmid.shape[0]
    tm = TM_MOE
    nt = t // tm
    grid_spec = pltpu.PrefetchScalarGridSpec(
        num_scalar_prefetch=1,
        grid=(nt,),
        in_specs=[pl.BlockSpec((tm, LANES), lambda i, tab: (i, 0)),
                  pl.BlockSpec((tm, D_MODEL), lambda i, tab: (i, 0)),
                  pl.BlockSpec(memory_space=pl.ANY)],
        out_specs=pl.BlockSpec((tm, D_MODEL), lambda i, tab: (i, 0)),
        scratch_shapes=[pltpu.VMEM((2, MOE_LOCAL, D_MODEL), BF16), pltpu.SemaphoreType.DMA((2,))],
    )
    return pl.pallas_call(
        _combine_kernel,
        grid_spec=grid_spec,
        out_shape=jax.ShapeDtypeStruct((t, D_MODEL), F32),
        compiler_params=_cparams(("arbitrary",)),
        name="combine",
    )(tab, meta, xmid, y)


def kernel(x, norm_mix, w_in, hg_lb, hg_out_norm, da_q_norm, da_k_norm, da_lambda, da_out_norm, w_branch_hg,
           w_branch_da, w_out, norm_moe, w_router_group, b_router_group, w_router_expert, b_router_expert,
           w1, w3, w2):
    batch, seq, d = x.shape
    assert d == D_MODEL and norm_mix.shape[0] == 1 and w_in.shape[2] == IN_COLS
    assert seq % TM_PROJ == 0 and seq % TQ_ATTN == 0 and (batch * seq) % TM_MOE == 0
    t = batch * seq
    x2 = x.reshape(t, d)

    half = DA_HEAD // 2
    inv = ROPE_THETA ** (-jnp.arange(half, dtype=F32) / half)
    ang = jnp.arange(seq, dtype=F32)[:, None] * inv[None, :]
    cos_t = jnp.tile(jnp.cos(ang), (1, 2 * LANES // DA_HEAD))
    sin_t = jnp.tile(jnp.concatenate([-jnp.sin(ang), jnp.sin(ang)], axis=1), (1, LANES // DA_HEAD))

    reps = DA_WIDTH // DA_HEAD
    gq = jnp.tile(da_q_norm[0].astype(F32) * (DA_HEAD ** -0.5 * math.log2(math.e)), reps)[None, :]
    gk = jnp.tile(da_k_norm[0].astype(F32), reps)[None, :]

    o_hg, qn, kn, dv, sgh, sgd = _in_proj(
        x2, norm_mix.astype(F32), w_in[0].astype(BF16), hg_lb.astype(F32), gq, gk, cos_t, sin_t,
        hg_out_norm.astype(F32), seq)
    o_da = _attn(da_lambda[0].astype(F32), qn, kn, dv, da_out_norm.astype(F32), batch, seq)

    wr = jnp.zeros((D_MODEL, LANES), F32)
    wr = wr.at[:, :N_GROUPS].set(w_router_group[0]).at[:, N_GROUPS:N_GROUPS + N_EXPERTS].set(w_router_expert[0])
    br = jnp.zeros((1, LANES), F32)
    br = br.at[0, :N_GROUPS].set(b_router_group[0]).at[0, N_GROUPS:N_GROUPS + N_EXPERTS].set(b_router_expert[0])

    xmid, xn, meta, metat, ttab, cnt = _merge(
        o_hg, o_da, sgh, sgd, x2, w_branch_hg[0].astype(BF16), w_branch_da[0].astype(BF16), w_out[0].astype(BF16),
        norm_moe.astype(F32), wr, br)

    tmx = TM_EXPERT
    nt = t // TM_MOE
    ttab = ttab.reshape(nt, SUBLANES, LANES)[:, :, :N_EXPERTS].astype(jnp.int32)
    counts = cnt[0, :N_EXPERTS].astype(jnp.int32)
    padded = (counts + tmx - 1) // tmx * tmx
    pad_end = jnp.cumsum(padded)
    pad_start = pad_end - padded
    gstart, seg_len, lstart = pad_start[None, :] + ttab[:, 0], ttab[:, 1], ttab[:, 2]
    lrow = (jnp.arange(TAB_COUNT, dtype=jnp.int32) * MOE_CHUNK)[None, :, None]
    in_seg = (lrow >= lstart[:, None, :]) & (lrow < (lstart + seg_len)[:, None, :])
    chunk_grow = jnp.sum(jnp.where(in_seg, gstart[:, None, :] + lrow - lstart[:, None, :], 0), axis=2)
    tab = jnp.concatenate([chunk_grow, jnp.sum(seg_len, axis=1, keepdims=True) // MOE_CHUNK,
                           jnp.zeros((nt, LANES - TAB_COUNT - 1), jnp.int32)], axis=1)
    tail = jnp.concatenate([pad_start + counts, (padded - counts) // MOE_CHUNK, pad_end[-1:] // tmx])
    n_rows = t * TOP_K + nt * N_EXPERTS * MOE_CHUNK + N_EXPERTS * tmx
    nb = n_rows // tmx
    block_start = jnp.arange(nb, dtype=jnp.int32) * tmx
    block_expert = jnp.minimum(jnp.sum(pad_end[None, :] <= block_start[:, None], axis=1),
                               N_EXPERTS - 1).astype(jnp.int32)
    n_active = (pad_end[-1:] // tmx).astype(jnp.int32)

    xg = _dispatch(tab, tail, metat, xn, n_rows)
    y = _experts(block_expert, n_active, xg, w1[0], w3[0], w2[0])
    out = _combine(tab, meta, xmid, y)
    return out.reshape(batch, seq, d)
```

```python
import functools
import math

import jax
import jax.numpy as jnp
from jax import lax
from jax.experimental import pallas as pl
from jax.experimental.pallas import tpu as pltpu

F32 = jnp.float32
BF16 = jnp.bfloat16

D_MODEL = 1024
HG_HEADS = 4
HG_DK = 128
HG_WIDTH = HG_HEADS * HG_DK
HG_CHUNK = 64
DA_HEADS = 4
DA_HEAD = 64
DA_VDIM = 2 * DA_HEAD
DA_WIDTH = DA_HEADS * DA_VDIM
ROPE_THETA = 10000.0
N_GROUPS = 4
EXPERTS_PER_GROUP = 8
N_EXPERTS = N_GROUPS * EXPERTS_PER_GROUP
TOP_K = 2
D_FF = 512
EPS = 1e-6
LAM_INIT = 0.8 - 0.6 * math.exp(-0.3 * 0)
IN_COLS = 4 * HG_WIDTH + 3 * DA_WIDTH + 2 * D_MODEL

LANES = 128
SUBLANES = 8
BF16_ROWS = 16
NEG = -1e30

TM_PROJ = 512
HGRN_GROUP = 256
TQ_ATTN = 512
ATTN_REGION = 9
TM_MOE = 512
TM_EXPERT = 512
MOE_CHUNK = BF16_ROWS
MOE_LOCAL = TOP_K * TM_MOE + N_EXPERTS * MOE_CHUNK
TAB_COUNT = MOE_LOCAL // MOE_CHUNK
MOE_DMA_GROUP = 8
MOE_KBLOCK = 256
VMEM_LIMIT = 56 * 1024 * 1024


def _cparams(sem):
    return pltpu.CompilerParams(dimension_semantics=sem, vmem_limit_bytes=VMEM_LIMIT)


def _sigmoid(v):
    return 1.0 / (1.0 + jnp.exp(-v))


def _qk_prep(d, gain, cos, sin, gmat):
    ss = jnp.dot((d * d).astype(BF16), gmat, preferred_element_type=F32)
    y = d * lax.rsqrt(ss * (1.0 / DA_HEAD) + EPS) * gain
    lane = lax.broadcasted_iota(jnp.int32, (d.shape[0], LANES), 1)
    upper = (lane & (DA_HEAD // 2)) != 0
    outs = []
    for c in range(d.shape[1] // LANES):
        yc = y[:, c * LANES:(c + 1) * LANES]
        sw = jnp.where(upper, pltpu.roll(yc, DA_HEAD // 2, 1), pltpu.roll(yc, LANES - DA_HEAD // 2, 1))
        outs.append(yc * cos + sw * sin)
    return jnp.concatenate(outs, axis=1)


def _in_proj_kernel(x_ref, nm_ref, w_ref, lb_ref, gq_ref, gk_ref, cos_ref, sin_ref, hgain_ref,
                    ohg_ref, qn_ref, kn_ref, dv_ref, sgh_ref, sgd_ref, st_ref, *, tiles_per_seq):
    @pl.when(pl.program_id(0) % tiles_per_seq == 0)
    def _():
        st_ref[...] = jnp.zeros_like(st_ref)

    x = x_ref[...]
    h = x * lax.rsqrt(jnp.mean(x * x, axis=-1, keepdims=True) + EPS) * nm_ref[...]
    hb = h.astype(BF16)

    def proj(c0, width):
        return jnp.dot(hb, w_ref[:, c0:c0 + width], preferred_element_type=F32)

    hq = proj(0, HG_WIDTH)
    hf = proj(HG_WIDTH, HG_WIDTH)
    lbp = lb_ref[...]
    mx = jnp.maximum(lbp[0:1], lbp[1:2])
    e0 = jnp.exp(lbp[0:1] - mx)
    e1 = jnp.exp(lbp[1:2] - mx)
    lb = e0 / (e0 + e1)
    f = lb + (1.0 - lb) * _sigmoid(hf)
    hi = proj(2 * HG_WIDTH, HG_WIDTH)
    hog = proj(3 * HG_WIDTH, HG_WIDTH)
    hgrn_head = _hgrn_tile(hq, jnp.log(f), 1.0 - f, hi, hog * _sigmoid(hog), hgain_ref[...], st_ref)

    def put_head(h):
        ohg_ref[:, h * HG_DK:(h + 1) * HG_DK] = hgrn_head(h)

    r = lax.broadcasted_iota(jnp.int32, (DA_WIDTH, DA_WIDTH), 0) // DA_HEAD
    c = lax.broadcasted_iota(jnp.int32, (DA_WIDTH, DA_WIDTH), 1) // DA_HEAD
    gmat = jnp.where(r == c, 1.0, 0.0).astype(BF16)
    cos = cos_ref[...]
    sin = sin_ref[...]
    base = 4 * HG_WIDTH
    qn_ref[...] = _qk_prep(proj(base, DA_WIDTH), gq_ref[...], cos, sin, gmat).astype(BF16)
    put_head(0)
    kn_ref[...] = _qk_prep(proj(base + DA_WIDTH, DA_WIDTH), gk_ref[...], cos, sin, gmat).astype(BF16)
    put_head(1)
    dv_ref[...] = proj(base + 2 * DA_WIDTH, DA_WIDTH).astype(BF16)
    base += 3 * DA_WIDTH
    sgh_ref[...] = _sigmoid(proj(base, D_MODEL)).astype(BF16)
    put_head(2)
    sgd_ref[...] = _sigmoid(proj(base + D_MODEL, D_MODEL)).astype(BF16)
    put_head(3)


def _in_proj(x2, norm_mix, w_in_bf, hg_lb, gq, gk, cos_t, sin_t, hg_gain, seq):
    t = x2.shape[0]
    tm = TM_PROJ
    nseq = seq // tm
    row = lambda w: pl.BlockSpec((tm, w), lambda i: (i, 0))
    const = lambda shape: pl.BlockSpec(shape, lambda i: (0, 0))
    tab = pl.BlockSpec((tm, LANES), lambda i: (i % nseq, 0))
    out_shape = [
        jax.ShapeDtypeStruct((t, HG_WIDTH), BF16),
        jax.ShapeDtypeStruct((t, DA_WIDTH), BF16),
        jax.ShapeDtypeStruct((t, DA_WIDTH), BF16),
        jax.ShapeDtypeStruct((t, DA_WIDTH), BF16),
        jax.ShapeDtypeStruct((t, D_MODEL), BF16),
        jax.ShapeDtypeStruct((t, D_MODEL), BF16),
    ]
    out_specs = [row(HG_WIDTH)] + [row(DA_WIDTH)] * 3 + [row(D_MODEL)] * 2
    return pl.pallas_call(
        functools.partial(_in_proj_kernel, tiles_per_seq=nseq),
        grid=(t // tm,),
        in_specs=[row(D_MODEL), const((1, D_MODEL)), const((D_MODEL, IN_COLS)), const((2, HG_WIDTH)),
                  const((1, DA_WIDTH)), const((1, DA_WIDTH)), tab, tab, const((1, HG_DK))],
        out_specs=out_specs,
        out_shape=out_shape,
        scratch_shapes=[pltpu.VMEM((HG_HEADS, HG_DK, HG_DK), F32)],
        compiler_params=_cparams(("arbitrary",)),
        name="in_proj",
    )(x2, norm_mix, w_in_bf, hg_lb, gq, gk, cos_t, sin_t, hg_gain)


def _hgrn_tile(q, g, k, v32, og, gain, st_ref):
    c = HG_CHUNK
    tc = q.shape[0]
    nc = tc // c
    grp = HGRN_GROUP
    row = lax.broadcasted_iota(jnp.int32, (grp, grp), 0)
    col = lax.broadcasted_iota(jnp.int32, (grp, grp), 1)
    causal = (row // c == col // c) & (col <= row)
    tri = jnp.where(causal, 1.0, 0.0).astype(BF16)
    nt = (((1,), (1,)), ((), ()))
    pair_lane = lax.broadcasted_iota(jnp.int32, (HG_DK, 2 * c), 1)
    in_chunk = [pair_lane < c, pair_lane >= c]

    g_hi = g.astype(BF16)
    g_lo = (g - g_hi.astype(F32)).astype(BF16)
    b = jnp.concatenate(
        [jnp.dot(tri, g_hi[r0:r0 + grp], preferred_element_type=F32)
         + jnp.dot(tri, g_lo[r0:r0 + grp], preferred_element_type=F32) for r0 in range(0, tc, grp)], axis=0)
    b_mid_rows = [b[ci * c + c // 2 - 1:ci * c + c // 2, :] for ci in range(nc)]
    b_end_rows = [b[ci * c + c - 1:ci * c + c, :] for ci in range(nc)]
    per_chunk = lambda rows: jnp.concatenate([jnp.broadcast_to(r, (c, r.shape[1])) for r in rows], axis=0)
    b_mid = per_chunk(b_mid_rows)
    e_fwd = jnp.exp(b - b_mid)
    e_bwd = jnp.exp(b_mid - b)
    q_rel32 = q * e_fwd
    k_rel32 = k * e_bwd
    q_rel = q_rel32.astype(BF16)
    k_rel = k_rel32.astype(BF16)
    q_st = (q_rel32 * per_chunk([jnp.exp(r) for r in b_mid_rows])).astype(BF16)
    k_end = (k_rel32 * per_chunk([jnp.exp(e - m) for e, m in zip(b_end_rows, b_mid_rows)])).astype(BF16)
    decay = [jnp.exp(r) for r in b_end_rows]
    v = v32.astype(BF16)

    def head(h):
        cols = slice(h * HG_DK, (h + 1) * HG_DK)
        intra = []
        for r0 in range(0, tc, grp):
            rows = slice(r0, r0 + grp)
            sc = lax.dot_general(q_rel[rows, cols], k_rel[rows, cols], nt, preferred_element_type=F32)
            sc = jnp.where(causal, sc, 0.0).astype(BF16)
            intra.append(jnp.dot(sc, v[rows, cols], preferred_element_type=F32))
        v_t = v32[:, cols].T.astype(BF16)
        incr = []
        for ci in range(nc):
            pair = slice((ci // 2) * 2 * c, (ci // 2 + 1) * 2 * c)
            lhs = jnp.where(in_chunk[ci % 2], v_t[:, pair], jnp.zeros((HG_DK, 2 * c), BF16))
            incr.append(jnp.dot(lhs, k_end[pair, cols], preferred_element_type=F32))
        st = st_ref[h]
        inter = []
        for ci in range(nc):
            rows = slice(ci * c, (ci + 1) * c)
            inter.append(lax.dot_general(q_st[rows, cols], st.astype(BF16), nt, preferred_element_type=F32))
            st = st * decay[ci][:, cols] + incr[ci]
        st_ref[h] = st
        o = jnp.concatenate(intra, axis=0) + jnp.concatenate(inter, axis=0)
        on = o * lax.rsqrt(jnp.mean(o * o, axis=-1, keepdims=True) + EPS) * gain
        return (on * og[:, cols]).astype(BF16)

    return head


def _attn_kernel(lam_ref, q_ref, k_ref, v_ref, gain_ref, o_ref, s_ref, m_ref, acc_ref):
    tq = TQ_ATTN
    dv = DA_VDIM
    nq = q_ref.shape[0] // tq
    nt = (((1,), (1,)), ((), ()))
    ones = jnp.ones((tq, dv), BF16)
    lane = lax.broadcasted_iota(jnp.int32, (tq, dv), 1)
    zero = jnp.zeros((tq, dv), BF16)
    r = lax.broadcasted_iota(jnp.int32, (2 * tq, tq), 0)
    c = lax.broadcasted_iota(jnp.int32, (2 * tq, tq), 1)
    on_or_below_diag = c <= jnp.where(r >= tq, r - tq, r)
    lp = lam_ref[...]
    lam = (jnp.exp(jnp.sum(lp[0:1] * lp[1:2], axis=1, keepdims=True))
           - jnp.exp(jnp.sum(lp[2:3] * lp[3:4], axis=1, keepdims=True)) + LAM_INIT)
    out_gain = gain_ref[...] * (1.0 - LAM_INIT)

    def block_rows(j):
        return slice(j * tq, (j + 1) * tq)

    def stacked_q(i):
        q = q_ref[block_rows(i), :]
        return jnp.concatenate([jnp.where(lane < DA_HEAD, q, zero), jnp.where(lane >= DA_HEAD, q, zero)], axis=0)

    def accumulate(s, j):
        v1 = jnp.concatenate([v_ref[block_rows(j), :], ones], axis=1)
        m_prev = m_ref[...]
        m_new = jnp.maximum(m_prev, jnp.max(s, axis=1, keepdims=True))
        alpha = jnp.exp2(m_prev - m_new)
        p = jnp.exp2(s - jnp.tile(m_new, (1, tq // LANES)))
        acc_ref[...] = (jnp.tile(alpha, (1, 2)) * acc_ref[...]
                        + jnp.dot(p.astype(BF16), v1, preferred_element_type=F32))
        m_ref[...] = m_new

    visits = [(i, j) for i in range(nq) for j in range(i + 1)]
    stacked = {}

    def put_scores(t):
        i, j = visits[t]
        if i not in stacked:
            stacked.clear()
            stacked[i] = stacked_q(i)
        s = lax.dot_general(stacked[i], k_ref[block_rows(j), :], nt, preferred_element_type=F32)
        s_ref[t % 2] = jnp.where(on_or_below_diag, s, NEG) if j == i else s

    def stage(t):
        i, j = visits[t]
        if t + 1 < len(visits):
            put_scores(t + 1)
        if j == 0:
            m_ref[...] = jnp.full_like(m_ref, NEG)
            acc_ref[...] = jnp.zeros_like(acc_ref)
        accumulate(s_ref[t % 2], j)
        if j == i:
            acc = acc_ref[...]
            o = acc[:tq, :dv] / acc[:tq, dv:] - lam * (acc[tq:, :dv] / acc[tq:, dv:])
            on = o * lax.rsqrt(jnp.mean(o * o, axis=-1, keepdims=True) + EPS) * out_gain
            o_ref[block_rows(i), :] = on.astype(BF16)

    put_scores(0)
    once = jnp.minimum(pl.program_id(0) + 1, 1)
    for t0 in range(0, len(visits), ATTN_REGION):
        def region(_, carry, t0=t0):
            stacked.clear()
            for t in range(t0, min(t0 + ATTN_REGION, len(visits))):
                stage(t)
            return carry

        lax.fori_loop(0, once, region, 0)
        stacked.clear()


def _attn(lam_p, qn, kn, dv, gain, batch, seq):
    t = qn.shape[0]
    tq = TQ_ATTN
    head = lambda b, h: (b, h)
    return pl.pallas_call(
        _attn_kernel,
        grid=(batch, DA_HEADS),
        in_specs=[pl.BlockSpec((4, DA_HEAD), lambda b, h: (0, 0)),
                  pl.BlockSpec((seq, DA_VDIM), head),
                  pl.BlockSpec((seq, DA_VDIM), head),
                  pl.BlockSpec((seq, DA_VDIM), head),
                  pl.BlockSpec((1, DA_VDIM), lambda b, h: (0, 0))],
        out_specs=pl.BlockSpec((seq, DA_VDIM), head),
        out_shape=jax.ShapeDtypeStruct((t, DA_WIDTH), BF16),
        scratch_shapes=[pltpu.VMEM((2, 2 * tq, tq), F32), pltpu.VMEM((2 * tq, LANES), F32),
                        pltpu.VMEM((2 * tq, 2 * DA_VDIM), F32)],
        compiler_params=_cparams(("parallel", "parallel")),
        name="attn",
    )(lam_p, qn, kn, dv, gain)


def _merge_kernel(ohg_ref, oda_ref, sgh_ref, sgd_ref, x_ref, wbh_ref, wbd_ref, wo_ref, nm_ref, wr_ref, br_ref,
                  xmid_ref, xn_ref, meta_ref, metat_ref, ttab_ref, cnt_ref, carry_ref, logit_ref):
    tm = x_ref.shape[0]
    i = pl.program_id(0)

    @pl.when(i == 0)
    def _():
        carry_ref[...] = jnp.zeros_like(carry_ref)
        logit_ref[...] = jnp.zeros_like(logit_ref)

    logit = logit_ref[...]

    y_hg = jnp.dot(ohg_ref[...], wbh_ref[...], preferred_element_type=F32)
    y_da = jnp.dot(oda_ref[...], wbd_ref[...], preferred_element_type=F32)
    mixed = sgh_ref[...].astype(F32) * y_hg + sgd_ref[...].astype(F32) * y_da
    xm = x_ref[...] + jnp.dot(mixed.astype(BF16), wo_ref[...], preferred_element_type=F32)
    xmid_ref[...] = xm
    xn = xm * lax.rsqrt(jnp.mean(xm * xm, axis=-1, keepdims=True) + EPS) * nm_ref[...]
    xn_ref[...] = xn.astype(BF16)
    logit_ref[...] = jnp.dot(xn.astype(BF16), wr_ref[...].astype(BF16), preferred_element_type=F32) + br_ref[...]

    lane = lax.broadcasted_iota(jnp.int32, (tm, LANES), 1)
    lanef = lane.astype(F32)
    big = float(LANES)
    is_g = lane < N_GROUPS
    gl = jnp.where(is_g, logit, NEG)
    gmax = jnp.max(gl, axis=1, keepdims=True)
    g_sel = jnp.min(jnp.where(gl == gmax, lanef, big), axis=1, keepdims=True)
    g_w = 1.0 / jnp.sum(jnp.where(is_g, jnp.exp(gl - gmax), 0.0), axis=1, keepdims=True)
    lo = N_GROUPS + EXPERTS_PER_GROUP * g_sel
    in_grp = (lanef >= lo) & (lanef < lo + EXPERTS_PER_GROUP)
    el = jnp.where(in_grp, logit, NEG)
    v1 = jnp.max(el, axis=1, keepdims=True)
    i1 = jnp.min(jnp.where(el == v1, lanef, big), axis=1, keepdims=True)
    el2 = jnp.where(lanef == i1, NEG, el)
    v2 = jnp.max(el2, axis=1, keepdims=True)
    i2 = jnp.min(jnp.where(el2 == v2, lanef, big), axis=1, keepdims=True)
    e1 = i1 - N_GROUPS
    e2 = i2 - N_GROUPS
    d = jnp.exp(v2 - v1)
    w1 = g_w / (1.0 + d)
    w2 = g_w * d / (1.0 + d)

    oh1 = jnp.where(lanef == e1, 1.0, 0.0)
    oh2 = jnp.where(lanef == e2, 1.0, 0.0)
    r = lax.broadcasted_iota(jnp.int32, (tm, tm), 0)
    c = lax.broadcasted_iota(jnp.int32, (tm, tm), 1)
    below = jnp.where(c < r, 1.0, 0.0).astype(BF16)
    c1 = jnp.dot(below, oh1.astype(BF16), preferred_element_type=F32)
    c2 = jnp.dot(below, oh2.astype(BF16), preferred_element_type=F32)
    tot1 = jnp.sum(oh1, axis=0, keepdims=True)
    tot2 = jnp.sum(oh2, axis=0, keepdims=True)
    n_chunk = jnp.floor((tot1 + tot2 + (MOE_CHUNK - 1)) * (1.0 / MOE_CHUNK))
    r = lax.broadcasted_iota(jnp.int32, (LANES, LANES), 0)
    c = lax.broadcasted_iota(jnp.int32, (LANES, LANES), 1)
    before = jnp.where(r < c, 1.0, 0.0).astype(BF16)
    lstart = MOE_CHUNK * jnp.dot(jnp.broadcast_to(n_chunk, (SUBLANES, LANES)).astype(BF16), before,
                                 preferred_element_type=F32)[0:1]
    lo1 = jnp.sum(oh1 * (c1 + lstart), axis=1, keepdims=True)
    lo2 = jnp.sum(oh2 * (c2 + lstart + tot1), axis=1, keepdims=True)
    carry = carry_ref[...]
    ttab_ref[...] = jnp.concatenate([carry, MOE_CHUNK * n_chunk, lstart, jnp.zeros((SUBLANES - 3, LANES), F32)], axis=0)
    carry = carry + jnp.where(i >= 1, MOE_CHUNK * n_chunk, 0.0)
    carry_ref[...] = carry
    cnt_ref[...] = jnp.broadcast_to(carry, cnt_ref.shape)

    meta = jnp.where(lane == 0, e1, jnp.where(lane == 1, e2, jnp.where(lane == 2, lo1, jnp.where(
        lane == 3, lo2, jnp.where(lane == 4, w1, jnp.where(lane == 5, w2, 0.0))))))
    meta_ref[...] = meta
    metat_ref[...] = meta.T[0:SUBLANES, :]


def _merge(o_hg, o_da, sgh, sgd, x2, wbh, wbd, wo, norm_moe, wr, br):
    t = x2.shape[0]
    tm = TM_MOE
    nt = t // tm
    cur = lambda i: jnp.minimum(i, nt - 1)
    prev = lambda i: jnp.maximum(i - 1, 0)
    row = lambda w: pl.BlockSpec((tm, w), lambda i: (cur(i), 0))
    const = lambda shape: pl.BlockSpec(shape, lambda i: (0, 0))
    return pl.pallas_call(
        _merge_kernel,
        grid=(nt + 1,),
        in_specs=[row(HG_WIDTH), row(DA_WIDTH), row(D_MODEL), row(D_MODEL), row(D_MODEL),
                  const((HG_WIDTH, D_MODEL)), const((DA_WIDTH, D_MODEL)), const((D_MODEL, D_MODEL)),
                  const((1, D_MODEL)), const((D_MODEL, LANES)), const((1, LANES))],
        out_specs=[row(D_MODEL), row(D_MODEL),
                   pl.BlockSpec((tm, LANES), lambda i: (prev(i), 0)),
                   pl.BlockSpec((SUBLANES, tm), lambda i: (0, prev(i))),
                   pl.BlockSpec((SUBLANES, LANES), lambda i: (prev(i), 0)),
                   const((SUBLANES, LANES))],
        out_shape=[jax.ShapeDtypeStruct((t, D_MODEL), F32),
                   jax.ShapeDtypeStruct((t, D_MODEL), BF16),
                   jax.ShapeDtypeStruct((t, LANES), F32),
                   jax.ShapeDtypeStruct((SUBLANES, t), F32),
                   jax.ShapeDtypeStruct((nt * SUBLANES, LANES), F32),
                   jax.ShapeDtypeStruct((SUBLANES, LANES), F32)],
        scratch_shapes=[pltpu.VMEM((1, LANES), F32), pltpu.VMEM((tm, LANES), F32)],
        compiler_params=_cparams(("arbitrary",)),
        name="merge",
    )(o_hg, o_da, sgh, sgd, x2, wbh, wbd, wo, norm_moe, wr, br)


def _for_each_chunk(tab_ref, tile, fn, group=MOE_DMA_GROUP):
    def one(c):
        fn(pl.multiple_of(c * MOE_CHUNK, MOE_CHUNK), pl.multiple_of(tab_ref[tile, c], MOE_CHUNK))

    def per_group(g, carry):
        for u in range(group):
            one(g * group + u)
        return carry

    def per_chunk(c, carry):
        one(c)
        return carry

    n = tab_ref[tile, TAB_COUNT]
    n_grouped = lax.div(n, group) * group
    lax.fori_loop(0, lax.div(n, group), per_group, 0)
    lax.fori_loop(n_grouped, n, per_chunk, 0)


def _dispatch_kernel(tab_ref, tail_ref, metat_ref, xn_ref, xg_hbm, xp_ref, zero_ref, sem, sem_z):
    i = pl.program_id(0)
    last = pl.num_programs(0) - 1
    lo1 = metat_ref[2:3, :]
    lo2 = metat_ref[3:4, :]
    xn = xn_ref[...]
    for r0 in range(0, MOE_LOCAL, MOE_KBLOCK):
        r = (lax.broadcasted_iota(jnp.int32, (MOE_KBLOCK, xn.shape[0]), 0) + r0).astype(F32)
        perm = jnp.where((r == lo1) | (r == lo2), 1.0, 0.0).astype(BF16)
        xp_ref[i % 2, r0:r0 + MOE_KBLOCK, :] = jnp.dot(perm, xn, preferred_element_type=F32).astype(BF16)

    def chunk_copy(tile, lrow, grow):
        return pltpu.make_async_copy(xp_ref.at[tile % 2, pl.ds(lrow, MOE_CHUNK), :],
                                     xg_hbm.at[pl.ds(grow, MOE_CHUNK), :], sem.at[tile % 2])

    _for_each_chunk(tab_ref, i, lambda lrow, grow: chunk_copy(i, lrow, grow).start())

    @pl.when(i > 0)
    def _():
        _for_each_chunk(tab_ref, i - 1, lambda lrow, grow: chunk_copy(i - 1, lrow, grow).wait())

    def tail_copy(e, c):
        grow = pl.multiple_of(tail_ref[e] + c * MOE_CHUNK, MOE_CHUNK)
        return pltpu.make_async_copy(zero_ref.at[pl.ds(0, MOE_CHUNK), :], xg_hbm.at[pl.ds(grow, MOE_CHUNK), :], sem_z)

    def for_each_tail(fn):
        def per_expert(e, carry):
            lax.fori_loop(0, tail_ref[N_EXPERTS + e], lambda c, c2: (fn(e, c), c2)[1], 0)
            return carry
        lax.fori_loop(0, N_EXPERTS, per_expert, 0)

    def block_copy(b):
        grow = pl.multiple_of(b * TM_EXPERT, TM_EXPERT)
        return pltpu.make_async_copy(zero_ref, xg_hbm.at[pl.ds(grow, TM_EXPERT), :], sem_z)

    def for_each_free_block(fn):
        lax.fori_loop(tail_ref[2 * N_EXPERTS], xg_hbm.shape[0] // TM_EXPERT, lambda b, c2: (fn(b), c2)[1], 0)

    @pl.when(i == last)
    def _():
        zero_ref[...] = jnp.zeros_like(zero_ref)
        for_each_tail(lambda e, c: tail_copy(e, c).start())
        for_each_free_block(lambda b: block_copy(b).start())
        for_each_tail(lambda e, c: tail_copy(e, c).wait())
        for_each_free_block(lambda b: block_copy(b).wait())
        _for_each_chunk(tab_ref, i, lambda lrow, grow: chunk_copy(i, lrow, grow).wait())


def _dispatch(tab, tail, metat, xn, n_rows):
    t = xn.shape[0]
    tm = TM_MOE
    grid_spec = pltpu.PrefetchScalarGridSpec(
        num_scalar_prefetch=2,
        grid=(t // tm,),
        in_specs=[pl.BlockSpec((SUBLANES, tm), lambda i, tab, tail: (0, i)),
                  pl.BlockSpec((tm, D_MODEL), lambda i, tab, tail: (i, 0))],
        out_specs=pl.BlockSpec(memory_space=pl.ANY),
        scratch_shapes=[pltpu.VMEM((2, MOE_LOCAL, D_MODEL), BF16), pltpu.VMEM((TM_EXPERT, D_MODEL), BF16),
                        pltpu.SemaphoreType.DMA((2,)), pltpu.SemaphoreType.DMA],
    )
    return pl.pallas_call(
        _dispatch_kernel,
        grid_spec=grid_spec,
        out_shape=jax.ShapeDtypeStruct((n_rows, D_MODEL), BF16),
        compiler_params=_cparams(("arbitrary",)),
        name="dispatch",
    )(tab, tail, metat, xn)


def _expert_kernel(be_ref, nact_ref, xg_ref, w1_ref, w3_ref, w2_ref, y_ref, w1b_ref, w3b_ref, w2b_ref):
    i = pl.program_id(0)
    active = i < nact_ref[0]

    @pl.when(active & ((i == 0) | (be_ref[i] != be_ref[jnp.maximum(i - 1, 0)])))
    def _():
        w1b_ref[...] = w1_ref[0].astype(BF16)
        w3b_ref[...] = w3_ref[0].astype(BF16)
        w2b_ref[...] = w2_ref[0].astype(BF16)

    @pl.when(active)
    def _():
        xb = xg_ref[...]
        a = jnp.dot(xb, w1b_ref[...], preferred_element_type=F32)
        b = jnp.dot(xb, w3b_ref[...], preferred_element_type=F32)
        hid = (a * _sigmoid(a) * b).astype(BF16)
        y_ref[...] = jnp.dot(hid, w2b_ref[...], preferred_element_type=F32).astype(BF16)

    @pl.when(jnp.logical_not(active))
    def _():
        y_ref[...] = jnp.zeros_like(y_ref)


def _experts(block_expert, n_active, xg, w1, w3, w2):
    tm = TM_EXPERT
    nb = xg.shape[0] // tm
    blk = lambda i, be, na: jnp.minimum(i, na[0] - 1)
    grid_spec = pltpu.PrefetchScalarGridSpec(
        num_scalar_prefetch=2,
        grid=(nb,),
        in_specs=[pl.BlockSpec((tm, D_MODEL), lambda i, be, na: (blk(i, be, na), 0)),
                  pl.BlockSpec((1, D_MODEL, D_FF), lambda i, be, na: (be[blk(i, be, na)], 0, 0)),
                  pl.BlockSpec((1, D_MODEL, D_FF), lambda i, be, na: (be[blk(i, be, na)], 0, 0)),
                  pl.BlockSpec((1, D_FF, D_MODEL), lambda i, be, na: (be[blk(i, be, na)], 0, 0))],
        out_specs=pl.BlockSpec((tm, D_MODEL), lambda i, be, na: (i, 0)),
        scratch_shapes=[pltpu.VMEM((D_MODEL, D_FF), BF16), pltpu.VMEM((D_MODEL, D_FF), BF16),
                        pltpu.VMEM((D_FF, D_MODEL), BF16)],
    )
    return pl.pallas_call(
        _expert_kernel,
        grid_spec=grid_spec,
        out_shape=jax.ShapeDtypeStruct(xg.shape, BF16),
        compiler_params=_cparams(("arbitrary",)),
        name="experts",
    )(block_expert, n_active, xg, w1, w3, w2)


def _combine_kernel(tab_ref, meta_ref, xmid_ref, y_hbm, o_ref, yl_ref, sem):
    tm = xmid_ref.shape[0]
    i = pl.program_id(0)

    def chunk_copy(tile, lrow, grow):
        return pltpu.make_async_copy(y_hbm.at[pl.ds(grow, MOE_CHUNK), :],
                                     yl_ref.at[tile % 2, pl.ds(lrow, MOE_CHUNK), :], sem.at[tile % 2])

    @pl.when(i == 0)
    def _():
        yl_ref[...] = jnp.zeros_like(yl_ref)
        _for_each_chunk(tab_ref, i, lambda lrow, grow: chunk_copy(i, lrow, grow).start())

    @pl.when(i + 1 < pl.num_programs(0))
    def _():
        _for_each_chunk(tab_ref, i + 1, lambda lrow, grow: chunk_copy(i + 1, lrow, grow).start())

    meta = meta_ref[...]
    lane = lax.broadcasted_iota(jnp.int32, (tm, MOE_LOCAL), 1).astype(F32)
    wsel = (jnp.where(lane == meta[:, 2:3], meta[:, 4:5], 0.0)
            + jnp.where(lane == meta[:, 3:4], meta[:, 5:6], 0.0)).astype(BF16)

    _for_each_chunk(tab_ref, i, lambda lrow, grow: chunk_copy(i, lrow, grow).wait())
    o_ref[...] = xmid_ref[...] + jnp.dot(wsel, yl_ref[i % 2], preferred_element_type=F32)


def _combine(tab, meta, xmid, y):
    t = xmid.shape[0]
    tm = TM_MOE
    nt = t // tm
    grid_spec = pltpu.PrefetchScalarGridSpec(
        num_scalar_prefetch=1,
        grid=(nt,),
        in_specs=[pl.BlockSpec((tm, LANES), lambda i, tab: (i, 0)),
                  pl.BlockSpec((tm, D_MODEL), lambda i, tab: (i, 0)),
                  pl.BlockSpec(memory_space=pl.ANY)],
        out_specs=pl.BlockSpec((tm, D_MODEL), lambda i, tab: (i, 0)),
        scratch_shapes=[pltpu.VMEM((2, MOE_LOCAL, D_MODEL), BF16), pltpu.SemaphoreType.DMA((2,))],
    )
    return pl.pallas_call(
        _combine_kernel,
        grid_spec=grid_spec,
        out_shape=jax.ShapeDtypeStruct((t, D_MODEL), F32),
        compiler_params=_cparams(("arbitrary",)),
        name="combine",
    )(tab, meta, xmid, y)


def kernel(x, norm_mix, w_in, hg_lb, hg_out_norm, da_q_norm, da_k_norm, da_lambda, da_out_norm, w_branch_hg,
           w_branch_da, w_out, norm_moe, w_router_group, b_router_group, w_router_expert, b_router_expert,
           w1, w3, w2):
    batch, seq, d = x.shape
    assert d == D_MODEL and norm_mix.shape[0] == 1 and w_in.shape[2] == IN_COLS
    assert seq % TM_PROJ == 0 and seq % TQ_ATTN == 0 and (batch * seq) % TM_MOE == 0
    t = batch * seq
    x2 = x.reshape(t, d)

    half = DA_HEAD // 2
    inv = ROPE_THETA ** (-jnp.arange(half, dtype=F32) / half)
    ang = jnp.arange(seq, dtype=F32)[:, None] * inv[None, :]
    cos_t = jnp.tile(jnp.cos(ang), (1, 2 * LANES // DA_HEAD))
    sin_t = jnp.tile(jnp.concatenate([-jnp.sin(ang), jnp.sin(ang)], axis=1), (1, LANES // DA_HEAD))

    reps = DA_WIDTH // DA_HEAD
    gq = jnp.tile(da_q_norm[0].astype(F32) * (DA_HEAD ** -0.5 * math.log2(math.e)), reps)[None, :]
    gk = jnp.tile(da_k_norm[0].astype(F32), reps)[None, :]

    o_hg, qn, kn, dv, sgh, sgd = _in_proj(
        x2, norm_mix.astype(F32), w_in[0].astype(BF16), hg_lb.astype(F32), gq, gk, cos_t, sin_t,
        hg_out_norm.astype(F32), seq)
    o_da = _attn(da_lambda[0].astype(F32), qn, kn, dv, da_out_norm.astype(F32), batch, seq)

    wr = jnp.zeros((D_MODEL, LANES), F32)
    wr = wr.at[:, :N_GROUPS].set(w_router_group[0]).at[:, N_GROUPS:N_GROUPS + N_EXPERTS].set(w_router_expert[0])
    br = jnp.zeros((1, LANES), F32)
    br = br.at[0, :N_GROUPS].set(b_router_group[0]).at[0, N_GROUPS:N_GROUPS + N_EXPERTS].set(b_router_expert[0])

    xmid, xn, meta, metat, ttab, cnt = _merge(
        o_hg, o_da, sgh, sgd, x2, w_branch_hg[0].astype(BF16), w_branch_da[0].astype(BF16), w_out[0].astype(BF16),
        norm_moe.astype(F32), wr, br)

    tmx = TM_EXPERT
    nt = t // TM_MOE
    ttab = ttab.reshape(nt, SUBLANES, LANES)[:, :, :N_EXPERTS].astype(jnp.int32)
    counts = cnt[0, :N_EXPERTS].astype(jnp.int32)
    padded = (counts + tmx - 1) // tmx * tmx
    pad_end = jnp.cumsum(padded)
    pad_start = pad_end - padded
    gstart, seg_len, lstart = pad_start[None, :] + ttab[:, 0], ttab[:, 1], ttab[:, 2]
    lrow = (jnp.arange(TAB_COUNT, dtype=jnp.int32) * MOE_CHUNK)[None, :, None]
    in_seg = (lrow >= lstart[:, None, :]) & (lrow < (lstart + seg_len)[:, None, :])
    chunk_grow = jnp.sum(jnp.where(in_seg, gstart[:, None, :] + lrow - lstart[:, None, :], 0), axis=2)
    tab = jnp.concatenate([chunk_grow, jnp.sum(seg_len, axis=1, keepdims=True) // MOE_CHUNK,
                           jnp.zeros((nt, LANES - TAB_COUNT - 1), jnp.int32)], axis=1)
    tail = jnp.concatenate([pad_start + counts, (padded - counts) // MOE_CHUNK, pad_end[-1:] // tmx])
    n_rows = t * TOP_K + nt * N_EXPERTS * MOE_CHUNK + N_EXPERTS * tmx
    nb = n_rows // tmx
    block_start = jnp.arange(nb, dtype=jnp.int32) * tmx
    block_expert = jnp.minimum(jnp.sum(pad_end[None, :] <= block_start[:, None], axis=1),
                               N_EXPERTS - 1).astype(jnp.int32)
    n_active = (pad_end[-1:] // tmx).astype(jnp.int32)

    xg = _dispatch(tab, tail, metat, xn, n_rows)
    y = _experts(block_expert, n_active, xg, w1[0], w3[0], w2[0])
    out = _combine(tab, meta, xmid, y)
    return out.reshape(batch, seq, d)
```

```python
import functools
import math

import jax
import jax.numpy as jnp
from jax import lax
from jax.experimental import pallas as pl
from jax.experimental.pallas import tpu as pltpu

F32 = jnp.float32
BF16 = jnp.bfloat16

D_MODEL = 1024
HG_HEADS = 4
HG_DK = 128
HG_WIDTH = HG_HEADS * HG_DK
HG_CHUNK = 64
DA_HEADS = 4
DA_HEAD = 64
DA_VDIM = 2 * DA_HEAD
DA_WIDTH = DA_HEADS * DA_VDIM
ROPE_THETA = 10000.0
N_GROUPS = 4
EXPERTS_PER_GROUP = 8
N_EXPERTS = N_GROUPS * EXPERTS_PER_GROUP
TOP_K = 2
D_FF = 512
EPS = 1e-6
LAM_INIT = 0.8 - 0.6 * math.exp(-0.3 * 0)
IN_COLS = 4 * HG_WIDTH + 3 * DA_WIDTH + 2 * D_MODEL

LANES = 128
SUBLANES = 8
BF16_ROWS = 16
NEG = -1e30

TM_PROJ = 512
HGRN_GROUP = 256
TQ_ATTN = 512
ATTN_REGION = 9
TM_MOE = 512
TM_EXPERT = 512
MOE_CHUNK = BF16_ROWS
MOE_LOCAL = TOP_K * TM_MOE + N_EXPERTS * MOE_CHUNK
TAB_COUNT = MOE_LOCAL // MOE_CHUNK
MOE_DMA_GROUP = 8
MOE_KBLOCK = 256
VMEM_LIMIT = 56 * 1024 * 1024


def _cparams(sem):
    return pltpu.CompilerParams(dimension_semantics=sem, vmem_limit_bytes=VMEM_LIMIT)


def _sigmoid(v):
    return 1.0 / (1.0 + jnp.exp(-v))


def _qk_prep(d, gain, cos, sin, gmat):
    ss = jnp.dot((d * d).astype(BF16), gmat, preferred_element_type=F32)
    y = d * lax.rsqrt(ss * (1.0 / DA_HEAD) + EPS) * gain
    lane = lax.broadcasted_iota(jnp.int32, (d.shape[0], LANES), 1)
    upper = (lane & (DA_HEAD // 2)) != 0
    outs = []
    for c in range(d.shape[1] // LANES):
        yc = y[:, c * LANES:(c + 1) * LANES]
        sw = jnp.where(upper, pltpu.roll(yc, DA_HEAD // 2, 1), pltpu.roll(yc, LANES - DA_HEAD // 2, 1))
        outs.append(yc * cos + sw * sin)
    return jnp.concatenate(outs, axis=1)


def _in_proj_kernel(x_ref, nm_ref, w_ref, lb_ref, gq_ref, gk_ref, cos_ref, sin_ref, hgain_ref,
                    ohg_ref, qn_ref, kn_ref, dv_ref, sgh_ref, sgd_ref, st_ref, *, tiles_per_seq):
    @pl.when(pl.program_id(0) % tiles_per_seq == 0)
    def _():
        st_ref[...] = jnp.zeros_like(st_ref)

    x = x_ref[...]
    h = x * lax.rsqrt(jnp.mean(x * x, axis=-1, keepdims=True) + EPS) * nm_ref[...]
    hb = h.astype(BF16)

    def proj(c0, width):
        return jnp.dot(hb, w_ref[:, c0:c0 + width], preferred_element_type=F32)

    hq = proj(0, HG_WIDTH)
    hf = proj(HG_WIDTH, HG_WIDTH)
    lbp = lb_ref[...]
    mx = jnp.maximum(lbp[0:1], lbp[1:2])
    e0 = jnp.exp(lbp[0:1] - mx)
    e1 = jnp.exp(lbp[1:2] - mx)
    lb = e0 / (e0 + e1)
    f = lb + (1.0 - lb) * _sigmoid(hf)
    hi = proj(2 * HG_WIDTH, HG_WIDTH)
    hog = proj(3 * HG_WIDTH, HG_WIDTH)
    hgrn_head = _hgrn_tile(hq, jnp.log(f), 1.0 - f, hi, hog * _sigmoid(hog), hgain_ref[...], st_ref)

    def put_head(h):
        ohg_ref[:, h * HG_DK:(h + 1) * HG_DK] = hgrn_head(h)

    r = lax.broadcasted_iota(jnp.int32, (DA_WIDTH, DA_WIDTH), 0) // DA_HEAD
    c = lax.broadcasted_iota(jnp.int32, (DA_WIDTH, DA_WIDTH), 1) // DA_HEAD
    gmat = jnp.where(r == c, 1.0, 0.0).astype(BF16)
    cos = cos_ref[...]
    sin = sin_ref[...]
    base = 4 * HG_WIDTH
    qn_ref[...] = _qk_prep(proj(base, DA_WIDTH), gq_ref[...], cos, sin, gmat).astype(BF16)
    put_head(0)
    kn_ref[...] = _qk_prep(proj(base + DA_WIDTH, DA_WIDTH), gk_ref[...], cos, sin, gmat).astype(BF16)
    put_head(1)
    dv_ref[...] = proj(base + 2 * DA_WIDTH, DA_WIDTH).astype(BF16)
    base += 3 * DA_WIDTH
    sgh_ref[...] = _sigmoid(proj(base, D_MODEL)).astype(BF16)
    put_head(2)
    sgd_ref[...] = _sigmoid(proj(base + D_MODEL, D_MODEL)).astype(BF16)
    put_head(3)


def _in_proj(x2, norm_mix, w_in_bf, hg_lb, gq, gk, cos_t, sin_t, hg_gain, seq):
    t = x2.shape[0]
    tm = TM_PROJ
    nseq = seq // tm
    row = lambda w: pl.BlockSpec((tm, w), lambda i: (i, 0))
    const = lambda shape: pl.BlockSpec(shape, lambda i: (0, 0))
    tab = pl.BlockSpec((tm, LANES), lambda i: (i % nseq, 0))
    out_shape = [
        jax.ShapeDtypeStruct((t, HG_WIDTH), BF16),
        jax.ShapeDtypeStruct((t, DA_WIDTH), BF16),
        jax.ShapeDtypeStruct((t, DA_WIDTH), BF16),
        jax.ShapeDtypeStruct((t, DA_WIDTH), BF16),
        jax.ShapeDtypeStruct((t, D_MODEL), BF16),
        jax.ShapeDtypeStruct((t, D_MODEL), BF16),
    ]
    out_specs = [row(HG_WIDTH)] + [row(DA_WIDTH)] * 3 + [row(D_MODEL)] * 2
    return pl.pallas_call(
        functools.partial(_in_proj_kernel, tiles_per_seq=nseq),
        grid=(t // tm,),
        in_specs=[row(D_MODEL), const((1, D_MODEL)), const((D_MODEL, IN_COLS)), const((2, HG_WIDTH)),
                  const((1, DA_WIDTH)), const((1, DA_WIDTH)), tab, tab, const((1, HG_DK))],
        out_specs=out_specs,
        out_shape=out_shape,
        scratch_shapes=[pltpu.VMEM((HG_HEADS, HG_DK, HG_DK), F32)],
        compiler_params=_cparams(("arbitrary",)),
        name="in_proj",
    )(x2, norm_mix, w_in_bf, hg_lb, gq, gk, cos_t, sin_t, hg_gain)


def _hgrn_tile(q, g, k, v32, og, gain, st_ref):
    c = HG_CHUNK
    tc = q.shape[0]
    nc = tc // c
    grp = HGRN_GROUP
    row = lax.broadcasted_iota(jnp.int32, (grp, grp), 0)
    col = lax.broadcasted_iota(jnp.int32, (grp, grp), 1)
    causal = (row // c == col // c) & (col <= row)
    tri = jnp.where(causal, 1.0, 0.0).astype(BF16)
    nt = (((1,), (1,)), ((), ()))
    pair_lane = lax.broadcasted_iota(jnp.int32, (HG_DK, 2 * c), 1)
    in_chunk = [pair_lane < c, pair_lane >= c]

    g_hi = g.astype(BF16)
    g_lo = (g - g_hi.astype(F32)).astype(BF16)
    b = jnp.concatenate(
        [jnp.dot(tri, g_hi[r0:r0 + grp], preferred_element_type=F32)
         + jnp.dot(tri, g_lo[r0:r0 + grp], preferred_element_type=F32) for r0 in range(0, tc, grp)], axis=0)
    b_mid_rows = [b[ci * c + c // 2 - 1:ci * c + c // 2, :] for ci in range(nc)]
    b_end_rows = [b[ci * c + c - 1:ci * c + c, :] for ci in range(nc)]
    per_chunk = lambda rows: jnp.concatenate([jnp.broadcast_to(r, (c, r.shape[1])) for r in rows], axis=0)
    b_mid = per_chunk(b_mid_rows)
    e_fwd = jnp.exp(b - b_mid)
    e_bwd = jnp.exp(b_mid - b)
    q_rel32 = q * e_fwd
    k_rel32 = k * e_bwd
    q_rel = q_rel32.astype(BF16)
    k_rel = k_rel32.astype(BF16)
    q_st = (q_rel32 * per_chunk([jnp.exp(r) for r in b_mid_rows])).astype(BF16)
    k_end = (k_rel32 * per_chunk([jnp.exp(e - m) for e, m in zip(b_end_rows, b_mid_rows)])).astype(BF16)
    decay = [jnp.exp(r) for r in b_end_rows]
    v = v32.astype(BF16)

    def head(h):
        cols = slice(h * HG_DK, (h + 1) * HG_DK)
        intra = []
        for r0 in range(0, tc, grp):
            rows = slice(r0, r0 + grp)
            sc = lax.dot_general(q_rel[rows, cols], k_rel[rows, cols], nt, preferred_element_type=F32)
            sc = jnp.where(causal, sc, 0.0).astype(BF16)
            intra.append(jnp.dot(sc, v[rows, cols], preferred_element_type=F32))
        v_t = v32[:, cols].T.astype(BF16)
        incr = []
        for ci in range(nc):
            pair = slice((ci // 2) * 2 * c, (ci // 2 + 1) * 2 * c)
            lhs = jnp.where(in_chunk[ci % 2], v_t[:, pair], jnp.zeros((HG_DK, 2 * c), BF16))
            incr.append(jnp.dot(lhs, k_end[pair, cols], preferred_element_type=F32))
        st = st_ref[h]
        inter = []
        for ci in range(nc):
            rows = slice(ci * c, (ci + 1) * c)
            inter.append(lax.dot_general(q_st[rows, cols], st.astype(BF16), nt, preferred_element_type=F32))
            st = st * decay[ci][:, cols] + incr[ci]
        st_ref[h] = st
        o = jnp.concatenate(intra, axis=0) + jnp.concatenate(inter, axis=0)
        on = o * lax.rsqrt(jnp.mean(o * o, axis=-1, keepdims=True) + EPS) * gain
        return (on * og[:, cols]).astype(BF16)

    return head


def _attn_kernel(lam_ref, q_ref, k_ref, v_ref, gain_ref, o_ref, s_ref, m_ref, acc_ref):
    tq = TQ_ATTN
    dv = DA_VDIM
    nq = q_ref.shape[0] // tq
    nt = (((1,), (1,)), ((), ()))
    ones = jnp.ones((tq, dv), BF16)
    lane = lax.broadcasted_iota(jnp.int32, (tq, dv), 1)
    zero = jnp.zeros((tq, dv), BF16)
    r = lax.broadcasted_iota(jnp.int32, (2 * tq, tq), 0)
    c = lax.broadcasted_iota(jnp.int32, (2 * tq, tq), 1)
    on_or_below_diag = c <= jnp.where(r >= tq, r - tq, r)
    lp = lam_ref[...]
    lam = (jnp.exp(jnp.sum(lp[0:1] * lp[1:2], axis=1, keepdims=True))
           - jnp.exp(jnp.sum(lp[2:3] * lp[3:4], axis=1, keepdims=True)) + LAM_INIT)
    out_gain = gain_ref[...] * (1.0 - LAM_INIT)

    def block_rows(j):
        return slice(j * tq, (j + 1) * tq)

    def stacked_q(i):
        q = q_ref[block_rows(i), :]
        return jnp.concatenate([jnp.where(lane < DA_HEAD, q, zero), jnp.where(lane >= DA_HEAD, q, zero)], axis=0)

    def accumulate(s, j):
        v1 = jnp.concatenate([v_ref[block_rows(j), :], ones], axis=1)
        m_prev = m_ref[...]
        m_new = jnp.maximum(m_prev, jnp.max(s, axis=1, keepdims=True))
        alpha = jnp.exp2(m_prev - m_new)
        p = jnp.exp2(s - jnp.tile(m_new, (1, tq // LANES)))
        acc_ref[...] = (jnp.tile(alpha, (1, 2)) * acc_ref[...]
                        + jnp.dot(p.astype(BF16), v1, preferred_element_type=F32))
        m_ref[...] = m_new

    visits = [(i, j) for i in range(nq) for j in range(i + 1)]
    stacked = {}

    def put_scores(t):
        i, j = visits[t]
        if i not in stacked:
            stacked.clear()
            stacked[i] = stacked_q(i)
        s = lax.dot_general(stacked[i], k_ref[block_rows(j), :], nt, preferred_element_type=F32)
        s_ref[t % 2] = jnp.where(on_or_below_diag, s, NEG) if j == i else s

    def stage(t):
        i, j = visits[t]
        if t + 1 < len(visits):
            put_scores(t + 1)
        if j == 0:
            m_ref[...] = jnp.full_like(m_ref, NEG)
            acc_ref[...] = jnp.zeros_like(acc_ref)
        accumulate(s_ref[t % 2], j)
        if j == i:
            acc = acc_ref[...]
            o = acc[:tq, :dv] / acc[:tq, dv:] - lam * (acc[tq:, :dv] / acc[tq:, dv:])
            on = o * lax.rsqrt(jnp.mean(o * o, axis=-1, keepdims=True) + EPS) * out_gain
            o_ref[block_rows(i), :] = on.astype(BF16)

    put_scores(0)
    once = jnp.minimum(pl.program_id(0) + 1, 1)
    for t0 in range(0, len(visits), ATTN_REGION):
        def region(_, carry, t0=t0):
            stacked.clear()
            for t in range(t0, min(t0 + ATTN_REGION, len(visits))):
                stage(t)
            return carry

        lax.fori_loop(0, once, region, 0)
        stacked.clear()


def _attn(lam_p, qn, kn, dv, gain, batch, seq):
    t = qn.shape[0]
    tq = TQ_ATTN
    head = lambda b, h: (b, h)
    return pl.pallas_call(
        _attn_kernel,
        grid=(batch, DA_HEADS),
        in_specs=[pl.BlockSpec((4, DA_HEAD), lambda b, h: (0, 0)),
                  pl.BlockSpec((seq, DA_VDIM), head),
                  pl.BlockSpec((seq, DA_VDIM), head),
                  pl.BlockSpec((seq, DA_VDIM), head),
                  pl.BlockSpec((1, DA_VDIM), lambda b, h: (0, 0))],
        out_specs=pl.BlockSpec((seq, DA_VDIM), head),
        out_shape=jax.ShapeDtypeStruct((t, DA_WIDTH), BF16),
        scratch_shapes=[pltpu.VMEM((2, 2 * tq, tq), F32), pltpu.VMEM((2 * tq, LANES), F32),
                        pltpu.VMEM((2 * tq, 2 * DA_VDIM), F32)],
        compiler_params=_cparams(("parallel", "parallel")),
        name="attn",
    )(lam_p, qn, kn, dv, gain)


def _merge_kernel(ohg_ref, oda_ref, sgh_ref, sgd_ref, x_ref, wbh_ref, wbd_ref, wo_ref, nm_ref, wr_ref, br_ref,
                  xmid_ref, xn_ref, meta_ref, metat_ref, ttab_ref, cnt_ref, carry_ref, logit_ref):
    tm = x_ref.shape[0]
    i = pl.program_id(0)

    @pl.when(i == 0)
    def _():
        carry_ref[...] = jnp.zeros_like(carry_ref)
        logit_ref[...] = jnp.zeros_like(logit_ref)

    logit = logit_ref[...]

    y_hg = jnp.dot(ohg_ref[...], wbh_ref[...], preferred_element_type=F32)
    y_da = jnp.dot(oda_ref[...], wbd_ref[...], preferred_element_type=F32)
    mixed = sgh_ref[...].astype(F32) * y_hg + sgd_ref[...].astype(F32) * y_da
    xm = x_ref[...] + jnp.dot(mixed.astype(BF16), wo_ref[...], preferred_element_type=F32)
    xmid_ref[...] = xm
    xn = xm * lax.rsqrt(jnp.mean(xm * xm, axis=-1, keepdims=True) + EPS) * nm_ref[...]
    xn_ref[...] = xn.astype(BF16)
    logit_ref[...] = jnp.dot(xn.astype(BF16), wr_ref[...].astype(BF16), preferred_element_type=F32) + br_ref[...]

    lane = lax.broadcasted_iota(jnp.int32, (tm, LANES), 1)
    lanef = lane.astype(F32)
    big = float(LANES)
    is_g = lane < N_GROUPS
    gl = jnp.where(is_g, logit, NEG)
    gmax = jnp.max(gl, axis=1, keepdims=True)
    g_sel = jnp.min(jnp.where(gl == gmax, lanef, big), axis=1, keepdims=True)
    g_w = 1.0 / jnp.sum(jnp.where(is_g, jnp.exp(gl - gmax), 0.0), axis=1, keepdims=True)
    lo = N_GROUPS + EXPERTS_PER_GROUP * g_sel
    in_grp = (lanef >= lo) & (lanef < lo + EXPERTS_PER_GROUP)
    el = jnp.where(in_grp, logit, NEG)
    v1 = jnp.max(el, axis=1, keepdims=True)
    i1 = jnp.min(jnp.where(el == v1, lanef, big), axis=1, keepdims=True)
    el2 = jnp.where(lanef == i1, NEG, el)
    v2 = jnp.max(el2, axis=1, keepdims=True)
    i2 = jnp.min(jnp.where(el2 == v2, lanef, big), axis=1, keepdims=True)
    e1 = i1 - N_GROUPS
    e2 = i2 - N_GROUPS
    d = jnp.exp(v2 - v1)
    w1 = g_w / (1.0 + d)
    w2 = g_w * d / (1.0 + d)

    oh1 = jnp.where(lanef == e1, 1.0, 0.0)
    oh2 = jnp.where(lanef == e2, 1.0, 0.0)
    r = lax.broadcasted_iota(jnp.int32, (tm, tm), 0)
    c = lax.broadcasted_iota(jnp.int32, (tm, tm), 1)
    below = jnp.where(c < r, 1.0, 0.0).astype(BF16)
    c1 = jnp.dot(below, oh1.astype(BF16), preferred_element_type=F32)
    c2 = jnp.dot(below, oh2.astype(BF16), preferred_element_type=F32)
    tot1 = jnp.sum(oh1, axis=0, keepdims=True)
    tot2 = jnp.sum(oh2, axis=0, keepdims=True)
    n_chunk = jnp.floor((tot1 + tot2 + (MOE_CHUNK - 1)) * (1.0 / MOE_CHUNK))
    r = lax.broadcasted_iota(jnp.int32, (LANES, LANES), 0)
    c = lax.broadcasted_iota(jnp.int32, (LANES, LANES), 1)
    before = jnp.where(r < c, 1.0, 0.0).astype(BF16)
    lstart = MOE_CHUNK * jnp.dot(jnp.broadcast_to(n_chunk, (SUBLANES, LANES)).astype(BF16), before,
                                 preferred_element_type=F32)[0:1]
    lo1 = jnp.sum(oh1 * (c1 + lstart), axis=1, keepdims=True)
    lo2 = jnp.sum(oh2 * (c2 + lstart + tot1), axis=1, keepdims=True)
    carry = carry_ref[...]
    ttab_ref[...] = jnp.concatenate([carry, MOE_CHUNK * n_chunk, lstart, jnp.zeros((SUBLANES - 3, LANES), F32)], axis=0)
    carry = carry + jnp.where(i >= 1, MOE_CHUNK * n_chunk, 0.0)
    carry_ref[...] = carry
    cnt_ref[...] = jnp.broadcast_to(carry, cnt_ref.shape)

    meta = jnp.where(lane == 0, e1, jnp.where(lane == 1, e2, jnp.where(lane == 2, lo1, jnp.where(
        lane == 3, lo2, jnp.where(lane == 4, w1, jnp.where(lane == 5, w2, 0.0))))))
    meta_ref[...] = meta
    metat_ref[...] = meta.T[0:SUBLANES, :]


def _merge(o_hg, o_da, sgh, sgd, x2, wbh, wbd, wo, norm_moe, wr, br):
    t = x2.shape[0]
    tm = TM_MOE
    nt = t // tm
    cur = lambda i: jnp.minimum(i, nt - 1)
    prev = lambda i: jnp.maximum(i - 1, 0)
    row = lambda w: pl.BlockSpec((tm, w), lambda i: (cur(i), 0))
    const = lambda shape: pl.BlockSpec(shape, lambda i: (0, 0))
    return pl.pallas_call(
        _merge_kernel,
        grid=(nt + 1,),
        in_specs=[row(HG_WIDTH), row(DA_WIDTH), row(D_MODEL), row(D_MODEL), row(D_MODEL),
                  const((HG_WIDTH, D_MODEL)), const((DA_WIDTH, D_MODEL)), const((D_MODEL, D_MODEL)),
                  const((1, D_MODEL)), const((D_MODEL, LANES)), const((1, LANES))],
        out_specs=[row(D_MODEL), row(D_MODEL),
                   pl.BlockSpec((tm, LANES), lambda i: (prev(i), 0)),
                   pl.BlockSpec((SUBLANES, tm), lambda i: (0, prev(i))),
                   pl.BlockSpec((SUBLANES, LANES), lambda i: (prev(i), 0)),
                   const((SUBLANES, LANES))],
        out_shape=[jax.ShapeDtypeStruct((t, D_MODEL), F32),
                   jax.ShapeDtypeStruct((t, D_MODEL), BF16),
                   jax.ShapeDtypeStruct((t, LANES), F32),
                   jax.ShapeDtypeStruct((SUBLANES, t), F32),
                   jax.ShapeDtypeStruct((nt * SUBLANES, LANES), F32),
                   jax.ShapeDtypeStruct((SUBLANES, LANES), F32)],
        scratch_shapes=[pltpu.VMEM((1, LANES), F32), pltpu.VMEM((tm, LANES), F32)],
        compiler_params=_cparams(("arbitrary",)),
        name="merge",
    )(o_hg, o_da, sgh, sgd, x2, wbh, wbd, wo, norm_moe, wr, br)


def _for_each_chunk(tab_ref, tile, fn, group=MOE_DMA_GROUP):
    def one(c):
        fn(pl.multiple_of(c * MOE_CHUNK, MOE_CHUNK), pl.multiple_of(tab_ref[tile, c], MOE_CHUNK))

    def per_group(g, carry):
        for u in range(group):
            one(g * group + u)
        return carry

    def per_chunk(c, carry):
        one(c)
        return carry

    n = tab_ref[tile, TAB_COUNT]
    n_grouped = lax.div(n, group) * group
    lax.fori_loop(0, lax.div(n, group), per_group, 0)
    lax.fori_loop(n_grouped, n, per_chunk, 0)


def _dispatch_kernel(tab_ref, tail_ref, metat_ref, xn_ref, xg_hbm, xp_ref, zero_ref, sem, sem_z):
    i = pl.program_id(0)
    last = pl.num_programs(0) - 1
    lo1 = metat_ref[2:3, :]
    lo2 = metat_ref[3:4, :]
    xn = xn_ref[...]
    for r0 in range(0, MOE_LOCAL, MOE_KBLOCK):
        r = (lax.broadcasted_iota(jnp.int32, (MOE_KBLOCK, xn.shape[0]), 0) + r0).astype(F32)
        perm = jnp.where((r == lo1) | (r == lo2), 1.0, 0.0).astype(BF16)
        xp_ref[i % 2, r0:r0 + MOE_KBLOCK, :] = jnp.dot(perm, xn, preferred_element_type=F32).astype(BF16)

    def chunk_copy(tile, lrow, grow):
        return pltpu.make_async_copy(xp_ref.at[tile % 2, pl.ds(lrow, MOE_CHUNK), :],
                                     xg_hbm.at[pl.ds(grow, MOE_CHUNK), :], sem.at[tile % 2])

    _for_each_chunk(tab_ref, i, lambda lrow, grow: chunk_copy(i, lrow, grow).start())

    @pl.when(i > 0)
    def _():
        _for_each_chunk(tab_ref, i - 1, lambda lrow, grow: chunk_copy(i - 1, lrow, grow).wait())

    def tail_copy(e, c):
        grow = pl.multiple_of(tail_ref[e] + c * MOE_CHUNK, MOE_CHUNK)
        return pltpu.make_async_copy(zero_ref.at[pl.ds(0, MOE_CHUNK), :], xg_hbm.at[pl.ds(grow, MOE_CHUNK), :], sem_z)

    def for_each_tail(fn):
        def per_expert(e, carry):
            lax.fori_loop(0, tail_ref[N_EXPERTS + e], lambda c, c2: (fn(e, c), c2)[1], 0)
            return carry
        lax.fori_loop(0, N_EXPERTS, per_expert, 0)

    def block_copy(b):
        grow = pl.multiple_of(b * TM_EXPERT, TM_EXPERT)
        return pltpu.make_async_copy(zero_ref, xg_hbm.at[pl.ds(grow, TM_EXPERT), :], sem_z)

    def for_each_free_block(fn):
        lax.fori_loop(tail_ref[2 * N_EXPERTS], xg_hbm.shape[0] // TM_EXPERT, lambda b, c2: (fn(b), c2)[1], 0)

    @pl.when(i == last)
    def _():
        zero_ref[...] = jnp.zeros_like(zero_ref)
        for_each_tail(lambda e, c: tail_copy(e, c).start())
        for_each_free_block(lambda b: block_copy(b).start())
        for_each_tail(lambda e, c: tail_copy(e, c).wait())
        for_each_free_block(lambda b: block_copy(b).wait())
        _for_each_chunk(tab_ref, i, lambda lrow, grow: chunk_copy(i, lrow, grow).wait())


def _dispatch(tab, tail, metat, xn, n_rows):
    t = xn.shape[0]
    tm = TM_MOE
    grid_spec = pltpu.PrefetchScalarGridSpec(
        num_scalar_prefetch=2,
        grid=(t // tm,),
        in_specs=[pl.BlockSpec((SUBLANES, tm), lambda i, tab, tail: (0, i)),
                  pl.BlockSpec((tm, D_MODEL), lambda i, tab, tail: (i, 0))],
        out_specs=pl.BlockSpec(memory_space=pl.ANY),
        scratch_shapes=[pltpu.VMEM((2, MOE_LOCAL, D_MODEL), BF16), pltpu.VMEM((TM_EXPERT, D_MODEL), BF16),
                        pltpu.SemaphoreType.DMA((2,)), pltpu.SemaphoreType.DMA],
    )
    return pl.pallas_call(
        _dispatch_kernel,
        grid_spec=grid_spec,
        out_shape=jax.ShapeDtypeStruct((n_rows, D_MODEL), BF16),
        compiler_params=_cparams(("arbitrary",)),
        name="dispatch",
    )(tab, tail, metat, xn)


def _expert_kernel(be_ref, nact_ref, xg_ref, w1_ref, w3_ref, w2_ref, y_ref, w1b_ref, w3b_ref, w2b_ref):
    i = pl.program_id(0)
    active = i < nact_ref[0]

    @pl.when(active & ((i == 0) | (be_ref[i] != be_ref[jnp.maximum(i - 1, 0)])))
    def _():
        w1b_ref[...] = w1_ref[0].astype(BF16)
        w3b_ref[...] = w3_ref[0].astype(BF16)
        w2b_ref[...] = w2_ref[0].astype(BF16)

    @pl.when(active)
    def _():
        xb = xg_ref[...]
        a = jnp.dot(xb, w1b_ref[...], preferred_element_type=F32)
        b = jnp.dot(xb, w3b_ref[...], preferred_element_type=F32)
        hid = (a * _sigmoid(a) * b).astype(BF16)
        y_ref[...] = jnp.dot(hid, w2b_ref[...], preferred_element_type=F32).astype(BF16)

    @pl.when(jnp.logical_not(active))
    def _():
        y_ref[...] = jnp.zeros_like(y_ref)


def _experts(block_expert, n_active, xg, w1, w3, w2):
    tm = TM_EXPERT
    nb = xg.shape[0] // tm
    blk = lambda i, be, na: jnp.minimum(i, na[0] - 1)
    grid_spec = pltpu.PrefetchScalarGridSpec(
        num_scalar_prefetch=2,
        grid=(nb,),
        in_specs=[pl.BlockSpec((tm, D_MODEL), lambda i, be, na: (blk(i, be, na), 0)),
                  pl.BlockSpec((1, D_MODEL, D_FF), lambda i, be, na: (be[blk(i, be, na)], 0, 0)),
                  pl.BlockSpec((1, D_MODEL, D_FF), lambda i, be, na: (be[blk(i, be, na)], 0, 0)),
                  pl.BlockSpec((1, D_FF, D_MODEL), lambda i, be, na: (be[blk(i, be, na)], 0, 0))],
        out_specs=pl.BlockSpec((tm, D_MODEL), lambda i, be, na: (i, 0)),
        scratch_shapes=[pltpu.VMEM((D_MODEL, D_FF), BF16), pltpu.VMEM((D_MODEL, D_FF), BF16),
                        pltpu.VMEM((D_FF, D_MODEL), BF16)],
    )
    return pl.pallas_call(
        _expert_kernel,
        grid_spec=grid_spec,
        out_shape=jax.ShapeDtypeStruct(xg.shape, BF16),
        compiler_params=_cparams(("arbitrary",)),
        name="experts",
    )(block_expert, n_active, xg, w1, w3, w2)


def _combine_kernel(tab_ref, meta_ref, xmid_ref, y_hbm, o_ref, yl_ref, sem):
    tm = xmid_ref.shape[0]
    i = pl.program_id(0)

    def chunk_copy(tile, lrow, grow):
        return pltpu.make_async_copy(y_hbm.at[pl.ds(grow, MOE_CHUNK), :],
                                     yl_ref.at[tile % 2, pl.ds(lrow, MOE_CHUNK), :], sem.at[tile % 2])

    @pl.when(i == 0)
    def _():
        yl_ref[...] = jnp.zeros_like(yl_ref)
        _for_each_chunk(tab_ref, i, lambda lrow, grow: chunk_copy(i, lrow, grow).start())

    @pl.when(i + 1 < pl.num_programs(0))
    def _():
        _for_each_chunk(tab_ref, i + 1, lambda lrow, grow: chunk_copy(i + 1, lrow, grow).start())

    meta = meta_ref[...]
    _for_each_chunk(tab_ref, i, lambda lrow, grow: chunk_copy(i, lrow, grow).wait())
    acc = xmid_ref[...]
    for k0 in range(0, MOE_LOCAL, MOE_KBLOCK):
        lane = (lax.broadcasted_iota(jnp.int32, (tm, MOE_KBLOCK), 1) + k0).astype(F32)
        wsel = (jnp.where(lane == meta[:, 2:3], meta[:, 4:5], 0.0)
                + jnp.where(lane == meta[:, 3:4], meta[:, 5:6], 0.0)).astype(BF16)
        acc = acc + jnp.dot(wsel, yl_ref[i % 2, k0:k0 + MOE_KBLOCK, :], preferred_element_type=F32)
    o_ref[...] = acc


def _combine(tab, meta, xmid, y):
    t = xmid.shape[0]
    tm = TM_MOE
    nt = t // tm
    grid_spec = pltpu.PrefetchScalarGridSpec(
        num_scalar_prefetch=1,
        grid=(nt,),
        in_specs=[pl.BlockSpec((tm, LANES), lambda i, tab: (i, 0)),
                  pl.BlockSpec((tm, D_MODEL), lambda i, tab: (i, 0)),
                  pl.BlockSpec(memory_space=pl.ANY)],
        out_specs=pl.BlockSpec((tm, D_MODEL), lambda i, tab: (i, 0)),
        scratch_shapes=[pltpu.VMEM((2, MOE_LOCAL, D_MODEL), BF16), pltpu.SemaphoreType.DMA((2,))],
    )
    return pl.pallas_call(
        _combine_kernel,
        grid_spec=grid_spec,
        out_shape=jax.ShapeDtypeStruct((t, D_MODEL), F32),
        compiler_params=_cparams(("arbitrary",)),
        name="combine",
    )(tab, meta, xmid, y)


def kernel(x, norm_mix, w_in, hg_lb, hg_out_norm, da_q_norm, da_k_norm, da_lambda, da_out_norm, w_branch_hg,
           w_branch_da, w_out, norm_moe, w_router_group, b_router_group, w_router_expert, b_router_expert,
           w1, w3, w2):
    batch, seq, d = x.shape
    assert d == D_MODEL and norm_mix.shape[0] == 1 and w_in.shape[2] == IN_COLS
    assert seq % TM_PROJ == 0 and seq % TQ_ATTN == 0 and (batch * seq) % TM_MOE == 0
    t = batch * seq
    x2 = x.reshape(t, d)

    half = DA_HEAD // 2
    inv = ROPE_THETA ** (-jnp.arange(half, dtype=F32) / half)
    ang = jnp.arange(seq, dtype=F32)[:, None] * inv[None, :]
    cos_t = jnp.tile(jnp.cos(ang), (1, 2 * LANES // DA_HEAD))
    sin_t = jnp.tile(jnp.concatenate([-jnp.sin(ang), jnp.sin(ang)], axis=1), (1, LANES // DA_HEAD))

    reps = DA_WIDTH // DA_HEAD
    gq = jnp.tile(da_q_norm[0].astype(F32) * (DA_HEAD ** -0.5 * math.log2(math.e)), reps)[None, :]
    gk = jnp.tile(da_k_norm[0].astype(F32), reps)[None, :]

    o_hg, qn, kn, dv, sgh, sgd = _in_proj(
        x2, norm_mix.astype(F32), w_in[0].astype(BF16), hg_lb.astype(F32), gq, gk, cos_t, sin_t,
        hg_out_norm.astype(F32), seq)
    o_da = _attn(da_lambda[0].astype(F32), qn, kn, dv, da_out_norm.astype(F32), batch, seq)

    wr = jnp.zeros((D_MODEL, LANES), F32)
    wr = wr.at[:, :N_GROUPS].set(w_router_group[0]).at[:, N_GROUPS:N_GROUPS + N_EXPERTS].set(w_router_expert[0])
    br = jnp.zeros((1, LANES), F32)
    br = br.at[0, :N_GROUPS].set(b_router_group[0]).at[0, N_GROUPS:N_GROUPS + N_EXPERTS].set(b_router_expert[0])

    xmid, xn, meta, metat, ttab, cnt = _merge(
        o_hg, o_da, sgh, sgd, x2, w_branch_hg[0].astype(BF16), w_branch_da[0].astype(BF16), w_out[0].astype(BF16),
        norm_moe.astype(F32), wr, br)

    tmx = TM_EXPERT
    nt = t // TM_MOE
    ttab = ttab.reshape(nt, SUBLANES, LANES)[:, :, :N_EXPERTS].astype(jnp.int32)
    counts = cnt[0, :N_EXPERTS].astype(jnp.int32)
    padded = (counts + tmx - 1) // tmx * tmx
    pad_end = jnp.cumsum(padded)
    pad_start = pad_end - padded
    gstart, seg_len, lstart = pad_start[None, :] + ttab[:, 0], ttab[:, 1], ttab[:, 2]
    lrow = (jnp.arange(TAB_COUNT, dtype=jnp.int32) * MOE_CHUNK)[None, :, None]
    in_seg = (lrow >= lstart[:, None, :]) & (lrow < (lstart + seg_len)[:, None, :])
    chunk_grow = jnp.sum(jnp.where(in_seg, gstart[:, None, :] + lrow - lstart[:, None, :], 0), axis=2)
    tab = jnp.concatenate([chunk_grow, jnp.sum(seg_len, axis=1, keepdims=True) // MOE_CHUNK,
                           jnp.zeros((nt, LANES - TAB_COUNT - 1), jnp.int32)], axis=1)
    tail = jnp.concatenate([pad_start + counts, (padded - counts) // MOE_CHUNK, pad_end[-1:] // tmx])
    n_rows = t * TOP_K + nt * N_EXPERTS * MOE_CHUNK + N_EXPERTS * tmx
    nb = n_rows // tmx
    block_start = jnp.arange(nb, dtype=jnp.int32) * tmx
    block_expert = jnp.minimum(jnp.sum(pad_end[None, :] <= block_start[:, None], axis=1),
                               N_EXPERTS - 1).astype(jnp.int32)
    n_active = (pad_end[-1:] // tmx).astype(jnp.int32)

    xg = _dispatch(tab, tail, metat, xn, n_rows)
    y = _experts(block_expert, n_active, xg, w1[0], w3[0], w2[0])
    out = _combine(tab, meta, xmid, y)
    return out.reshape(batch, seq, d)
```

```python
import functools
import math

import jax
import jax.numpy as jnp
from jax import lax
from jax.experimental import pallas as pl
from jax.experimental.pallas import tpu as pltpu

F32 = jnp.float32
BF16 = jnp.bfloat16

D_MODEL = 1024
HG_HEADS = 4
HG_DK = 128
HG_WIDTH = HG_HEADS * HG_DK
HG_CHUNK = 64
DA_HEADS = 4
DA_HEAD = 64
DA_VDIM = 2 * DA_HEAD
DA_WIDTH = DA_HEADS * DA_VDIM
ROPE_THETA = 10000.0
N_GROUPS = 4
EXPERTS_PER_GROUP = 8
N_EXPERTS = N_GROUPS * EXPERTS_PER_GROUP
TOP_K = 2
D_FF = 512
EPS = 1e-6
LAM_INIT = 0.8 - 0.6 * math.exp(-0.3 * 0)
IN_COLS = 4 * HG_WIDTH + 3 * DA_WIDTH + 2 * D_MODEL

LANES = 128
SUBLANES = 8
BF16_ROWS = 16
NEG = -1e30

TM_PROJ = 512
HGRN_GROUP = 256
TQ_ATTN = 512
ATTN_REGION = 9
TM_MOE = 512
TM_EXPERT = 512
MOE_CHUNK = BF16_ROWS
MOE_LOCAL = TOP_K * TM_MOE + N_EXPERTS * MOE_CHUNK
TAB_COUNT = MOE_LOCAL // MOE_CHUNK
MOE_DMA_GROUP = 8
MOE_KBLOCK = 256
VMEM_LIMIT = 56 * 1024 * 1024


def _cparams(sem):
    return pltpu.CompilerParams(dimension_semantics=sem, vmem_limit_bytes=VMEM_LIMIT)


def _sigmoid(v):
    return 1.0 / (1.0 + jnp.exp(-v))


def _qk_prep(d, gain, cos, sin, gmat):
    ss = jnp.dot((d * d).astype(BF16), gmat, preferred_element_type=F32)
    y = d * lax.rsqrt(ss * (1.0 / DA_HEAD) + EPS) * gain
    lane = lax.broadcasted_iota(jnp.int32, (d.shape[0], LANES), 1)
    upper = (lane & (DA_HEAD // 2)) != 0
    outs = []
    for c in range(d.shape[1] // LANES):
        yc = y[:, c * LANES:(c + 1) * LANES]
        sw = jnp.where(upper, pltpu.roll(yc, DA_HEAD // 2, 1), pltpu.roll(yc, LANES - DA_HEAD // 2, 1))
        outs.append(yc * cos + sw * sin)
    return jnp.concatenate(outs, axis=1)


def _in_proj_kernel(x_ref, nm_ref, w_ref, lb_ref, gq_ref, gk_ref, cos_ref, sin_ref, hgain_ref,
                    ohg_ref, qn_ref, kn_ref, dv_ref, sgh_ref, sgd_ref, st_ref, *, tiles_per_seq):
    @pl.when(pl.program_id(0) % tiles_per_seq == 0)
    def _():
        st_ref[...] = jnp.zeros_like(st_ref)

    x = x_ref[...]
    h = x * lax.rsqrt(jnp.mean(x * x, axis=-1, keepdims=True) + EPS) * nm_ref[...]
    hb = h.astype(BF16)

    def proj(c0, width):
        return jnp.dot(hb, w_ref[:, c0:c0 + width], preferred_element_type=F32)

    hq = proj(0, HG_WIDTH)
    hf = proj(HG_WIDTH, HG_WIDTH)
    lbp = lb_ref[...]
    mx = jnp.maximum(lbp[0:1], lbp[1:2])
    e0 = jnp.exp(lbp[0:1] - mx)
    e1 = jnp.exp(lbp[1:2] - mx)
    lb = e0 / (e0 + e1)
    f = lb + (1.0 - lb) * _sigmoid(hf)
    hi = proj(2 * HG_WIDTH, HG_WIDTH)
    hog = proj(3 * HG_WIDTH, HG_WIDTH)
    hgrn_head = _hgrn_tile(hq, jnp.log(f), 1.0 - f, hi, hog * _sigmoid(hog), hgain_ref[...], st_ref)

    def put_head(h):
        ohg_ref[:, h * HG_DK:(h + 1) * HG_DK] = hgrn_head(h)

    r = lax.broadcasted_iota(jnp.int32, (DA_WIDTH, DA_WIDTH), 0) // DA_HEAD
    c = lax.broadcasted_iota(jnp.int32, (DA_WIDTH, DA_WIDTH), 1) // DA_HEAD
    gmat = jnp.where(r == c, 1.0, 0.0).astype(BF16)
    cos = cos_ref[...]
    sin = sin_ref[...]
    base = 4 * HG_WIDTH
    qn_ref[...] = _qk_prep(proj(base, DA_WIDTH), gq_ref[...], cos, sin, gmat).astype(BF16)
    put_head(0)
    kn_ref[...] = _qk_prep(proj(base + DA_WIDTH, DA_WIDTH), gk_ref[...], cos, sin, gmat).astype(BF16)
    put_head(1)
    dv_ref[...] = proj(base + 2 * DA_WIDTH, DA_WIDTH).astype(BF16)
    base += 3 * DA_WIDTH
    sgh_ref[...] = _sigmoid(proj(base, D_MODEL)).astype(BF16)
    put_head(2)
    sgd_ref[...] = _sigmoid(proj(base + D_MODEL, D_MODEL)).astype(BF16)
    put_head(3)


def _in_proj(x2, norm_mix, w_in_bf, hg_lb, gq, gk, cos_t, sin_t, hg_gain, seq):
    t = x2.shape[0]
    tm = TM_PROJ
    nseq = seq // tm
    row = lambda w: pl.BlockSpec((tm, w), lambda i: (i, 0))
    const = lambda shape: pl.BlockSpec(shape, lambda i: (0, 0))
    tab = pl.BlockSpec((tm, LANES), lambda i: (i % nseq, 0))
    out_shape = [
        jax.ShapeDtypeStruct((t, HG_WIDTH), BF16),
        jax.ShapeDtypeStruct((t, DA_WIDTH), BF16),
        jax.ShapeDtypeStruct((t, DA_WIDTH), BF16),
        jax.ShapeDtypeStruct((t, DA_WIDTH), BF16),
        jax.ShapeDtypeStruct((t, D_MODEL), BF16),
        jax.ShapeDtypeStruct((t, D_MODEL), BF16),
    ]
    out_specs = [row(HG_WIDTH)] + [row(DA_WIDTH)] * 3 + [row(D_MODEL)] * 2
    return pl.pallas_call(
        functools.partial(_in_proj_kernel, tiles_per_seq=nseq),
        grid=(t // tm,),
        in_specs=[row(D_MODEL), const((1, D_MODEL)), const((D_MODEL, IN_COLS)), const((2, HG_WIDTH)),
                  const((1, DA_WIDTH)), const((1, DA_WIDTH)), tab, tab, const((1, HG_DK))],
        out_specs=out_specs,
        out_shape=out_shape,
        scratch_shapes=[pltpu.VMEM((HG_HEADS, HG_DK, HG_DK), F32)],
        compiler_params=_cparams(("arbitrary",)),
        name="in_proj",
    )(x2, norm_mix, w_in_bf, hg_lb, gq, gk, cos_t, sin_t, hg_gain)


def _hgrn_tile(q, g, k, v32, og, gain, st_ref):
    c = HG_CHUNK
    tc = q.shape[0]
    nc = tc // c
    grp = HGRN_GROUP
    row = lax.broadcasted_iota(jnp.int32, (grp, grp), 0)
    col = lax.broadcasted_iota(jnp.int32, (grp, grp), 1)
    causal = (row // c == col // c) & (col <= row)
    tri = jnp.where(causal, 1.0, 0.0).astype(BF16)
    nt = (((1,), (1,)), ((), ()))
    pair_lane = lax.broadcasted_iota(jnp.int32, (HG_DK, 2 * c), 1)
    in_chunk = [pair_lane < c, pair_lane >= c]

    g_hi = g.astype(BF16)
    g_lo = (g - g_hi.astype(F32)).astype(BF16)
    b = jnp.concatenate(
        [jnp.dot(tri, g_hi[r0:r0 + grp], preferred_element_type=F32)
         + jnp.dot(tri, g_lo[r0:r0 + grp], preferred_element_type=F32) for r0 in range(0, tc, grp)], axis=0)
    b_mid_rows = [b[ci * c + c // 2 - 1:ci * c + c // 2, :] for ci in range(nc)]
    b_end_rows = [b[ci * c + c - 1:ci * c + c, :] for ci in range(nc)]
    per_chunk = lambda rows: jnp.concatenate([jnp.broadcast_to(r, (c, r.shape[1])) for r in rows], axis=0)
    b_mid = per_chunk(b_mid_rows)
    e_fwd = jnp.exp(b - b_mid)
    e_bwd = jnp.exp(b_mid - b)
    q_rel32 = q * e_fwd
    k_rel32 = k * e_bwd
    q_rel = q_rel32.astype(BF16)
    k_rel = k_rel32.astype(BF16)
    q_st = (q_rel32 * per_chunk([jnp.exp(r) for r in b_mid_rows])).astype(BF16)
    k_end = (k_rel32 * per_chunk([jnp.exp(e - m) for e, m in zip(b_end_rows, b_mid_rows)])).astype(BF16)
    decay = [jnp.exp(r) for r in b_end_rows]
    v = v32.astype(BF16)

    def head(h):
        cols = slice(h * HG_DK, (h + 1) * HG_DK)
        intra = []
        for r0 in range(0, tc, grp):
            rows = slice(r0, r0 + grp)
            sc = lax.dot_general(q_rel[rows, cols], k_rel[rows, cols], nt, preferred_element_type=F32)
            sc = jnp.where(causal, sc, 0.0).astype(BF16)
            intra.append(jnp.dot(sc, v[rows, cols], preferred_element_type=F32))
        v_t = v32[:, cols].T.astype(BF16)
        incr = []
        for ci in range(nc):
            pair = slice((ci // 2) * 2 * c, (ci // 2 + 1) * 2 * c)
            lhs = jnp.where(in_chunk[ci % 2], v_t[:, pair], jnp.zeros((HG_DK, 2 * c), BF16))
            incr.append(jnp.dot(lhs, k_end[pair, cols], preferred_element_type=F32))
        st = st_ref[h]
        inter = []
        for ci in range(nc):
            rows = slice(ci * c, (ci + 1) * c)
            inter.append(lax.dot_general(q_st[rows, cols], st.astype(BF16), nt, preferred_element_type=F32))
            st = st * decay[ci][:, cols] + incr[ci]
        st_ref[h] = st
        o = jnp.concatenate(intra, axis=0) + jnp.concatenate(inter, axis=0)
        on = o * lax.rsqrt(jnp.mean(o * o, axis=-1, keepdims=True) + EPS) * gain
        return (on * og[:, cols]).astype(BF16)

    return head


def _attn_kernel(lam_ref, q_ref, k_ref, v_ref, gain_ref, o_ref, s_ref, m_ref, acc_ref):
    tq = TQ_ATTN
    dv = DA_VDIM
    nq = q_ref.shape[0] // tq
    nt = (((1,), (1,)), ((), ()))
    ones = jnp.ones((tq, dv), BF16)
    lane = lax.broadcasted_iota(jnp.int32, (tq, dv), 1)
    zero = jnp.zeros((tq, dv), BF16)
    r = lax.broadcasted_iota(jnp.int32, (2 * tq, tq), 0)
    c = lax.broadcasted_iota(jnp.int32, (2 * tq, tq), 1)
    on_or_below_diag = c <= jnp.where(r >= tq, r - tq, r)
    lp = lam_ref[...]
    lam = (jnp.exp(jnp.sum(lp[0:1] * lp[1:2], axis=1, keepdims=True))
           - jnp.exp(jnp.sum(lp[2:3] * lp[3:4], axis=1, keepdims=True)) + LAM_INIT)
    out_gain = gain_ref[...] * (1.0 - LAM_INIT)

    def block_rows(j):
        return slice(j * tq, (j + 1) * tq)

    def stacked_q(i):
        q = q_ref[block_rows(i), :]
        return jnp.concatenate([jnp.where(lane < DA_HEAD, q, zero), jnp.where(lane >= DA_HEAD, q, zero)], axis=0)

    def accumulate(s, j):
        v1 = jnp.concatenate([v_ref[block_rows(j), :], ones], axis=1)
        m_prev = m_ref[...]
        m_new = jnp.maximum(m_prev, jnp.max(s, axis=1, keepdims=True))
        alpha = jnp.exp2(m_prev - m_new)
        p = jnp.exp2(s - jnp.tile(m_new, (1, tq // LANES)))
        acc_ref[...] = (jnp.tile(alpha, (1, 2)) * acc_ref[...]
                        + jnp.dot(p.astype(BF16), v1, preferred_element_type=F32))
        m_ref[...] = m_new

    visits = [(i, j) for i in range(nq) for j in range(i + 1)]
    stacked = {}

    def put_scores(t):
        i, j = visits[t]
        if i not in stacked:
            stacked.clear()
            stacked[i] = stacked_q(i)
        s = lax.dot_general(stacked[i], k_ref[block_rows(j), :], nt, preferred_element_type=F32)
        s_ref[t % 2] = jnp.where(on_or_below_diag, s, NEG) if j == i else s

    def stage(t):
        i, j = visits[t]
        if t + 1 < len(visits):
            put_scores(t + 1)
        if j == 0:
            m_ref[...] = jnp.full_like(m_ref, NEG)
            acc_ref[...] = jnp.zeros_like(acc_ref)
        accumulate(s_ref[t % 2], j)
        if j == i:
            acc = acc_ref[...]
            o = acc[:tq, :dv] / acc[:tq, dv:] - lam * (acc[tq:, :dv] / acc[tq:, dv:])
            on = o * lax.rsqrt(jnp.mean(o * o, axis=-1, keepdims=True) + EPS) * out_gain
            o_ref[block_rows(i), :] = on.astype(BF16)

    put_scores(0)
    once = jnp.minimum(pl.program_id(0) + 1, 1)
    for t0 in range(0, len(visits), ATTN_REGION):
        def region(_, carry, t0=t0):
            stacked.clear()
            for t in range(t0, min(t0 + ATTN_REGION, len(visits))):
                stage(t)
            return carry

        lax.fori_loop(0, once, region, 0)
        stacked.clear()


def _attn(lam_p, qn, kn, dv, gain, batch, seq):
    t = qn.shape[0]
    tq = TQ_ATTN
    head = lambda b, h: (b, h)
    return pl.pallas_call(
        _attn_kernel,
        grid=(batch, DA_HEADS),
        in_specs=[pl.BlockSpec((4, DA_HEAD), lambda b, h: (0, 0)),
                  pl.BlockSpec((seq, DA_VDIM), head),
                  pl.BlockSpec((seq, DA_VDIM), head),
                  pl.BlockSpec((seq, DA_VDIM), head),
                  pl.BlockSpec((1, DA_VDIM), lambda b, h: (0, 0))],
        out_specs=pl.BlockSpec((seq, DA_VDIM), head),
        out_shape=jax.ShapeDtypeStruct((t, DA_WIDTH), BF16),
        scratch_shapes=[pltpu.VMEM((2, 2 * tq, tq), F32), pltpu.VMEM((2 * tq, LANES), F32),
                        pltpu.VMEM((2 * tq, 2 * DA_VDIM), F32)],
        compiler_params=_cparams(("parallel", "parallel")),
        name="attn",
    )(lam_p, qn, kn, dv, gain)


def _merge_kernel(ohg_ref, oda_ref, sgh_ref, sgd_ref, x_ref, wbh_ref, wbd_ref, wo_ref, nm_ref, wr_ref, br_ref,
                  xmid_ref, xn_ref, meta_ref, metat_ref, ttab_ref, cnt_ref, carry_ref, logit_ref):
    tm = x_ref.shape[0]
    i = pl.program_id(0)

    @pl.when(i == 0)
    def _():
        carry_ref[...] = jnp.zeros_like(carry_ref)
        logit_ref[...] = jnp.zeros_like(logit_ref)

    logit = logit_ref[...]

    y_hg = jnp.dot(ohg_ref[...], wbh_ref[...], preferred_element_type=F32)
    y_da = jnp.dot(oda_ref[...], wbd_ref[...], preferred_element_type=F32)
    mixed = sgh_ref[...].astype(F32) * y_hg + sgd_ref[...].astype(F32) * y_da
    xm = x_ref[...] + jnp.dot(mixed.astype(BF16), wo_ref[...], preferred_element_type=F32)
    xmid_ref[...] = xm
    xn = xm * lax.rsqrt(jnp.mean(xm * xm, axis=-1, keepdims=True) + EPS) * nm_ref[...]
    xn_ref[...] = xn.astype(BF16)
    logit_ref[...] = jnp.dot(xn.astype(BF16), wr_ref[...].astype(BF16), preferred_element_type=F32) + br_ref[...]

    lane = lax.broadcasted_iota(jnp.int32, (tm, LANES), 1)
    lanef = lane.astype(F32)
    big = float(LANES)
    is_g = lane < N_GROUPS
    gl = jnp.where(is_g, logit, NEG)
    gmax = jnp.max(gl, axis=1, keepdims=True)
    g_sel = jnp.min(jnp.where(gl == gmax, lanef, big), axis=1, keepdims=True)
    g_w = 1.0 / jnp.sum(jnp.where(is_g, jnp.exp(gl - gmax), 0.0), axis=1, keepdims=True)
    lo = N_GROUPS + EXPERTS_PER_GROUP * g_sel
    in_grp = (lanef >= lo) & (lanef < lo + EXPERTS_PER_GROUP)
    el = jnp.where(in_grp, logit, NEG)
    v1 = jnp.max(el, axis=1, keepdims=True)
    i1 = jnp.min(jnp.where(el == v1, lanef, big), axis=1, keepdims=True)
    el2 = jnp.where(lanef == i1, NEG, el)
    v2 = jnp.max(el2, axis=1, keepdims=True)
    i2 = jnp.min(jnp.where(el2 == v2, lanef, big), axis=1, keepdims=True)
    e1 = i1 - N_GROUPS
    e2 = i2 - N_GROUPS
    d = jnp.exp(v2 - v1)
    w1 = g_w / (1.0 + d)
    w2 = g_w * d / (1.0 + d)

    oh1 = jnp.where(lanef == e1, 1.0, 0.0)
    oh2 = jnp.where(lanef == e2, 1.0, 0.0)
    r = lax.broadcasted_iota(jnp.int32, (tm, tm), 0)
    c = lax.broadcasted_iota(jnp.int32, (tm, tm), 1)
    below = jnp.where(c < r, 1.0, 0.0).astype(BF16)
    c1 = jnp.dot(below, oh1.astype(BF16), preferred_element_type=F32)
    c2 = jnp.dot(below, oh2.astype(BF16), preferred_element_type=F32)
    tot1 = jnp.sum(oh1, axis=0, keepdims=True)
    tot2 = jnp.sum(oh2, axis=0, keepdims=True)
    n_chunk = jnp.floor((tot1 + tot2 + (MOE_CHUNK - 1)) * (1.0 / MOE_CHUNK))
    r = lax.broadcasted_iota(jnp.int32, (LANES, LANES), 0)
    c = lax.broadcasted_iota(jnp.int32, (LANES, LANES), 1)
    before = jnp.where(r < c, 1.0, 0.0).astype(BF16)
    lstart = MOE_CHUNK * jnp.dot(jnp.broadcast_to(n_chunk, (SUBLANES, LANES)).astype(BF16), before,
                                 preferred_element_type=F32)[0:1]
    lo1 = jnp.sum(oh1 * (c1 + lstart), axis=1, keepdims=True)
    lo2 = jnp.sum(oh2 * (c2 + lstart + tot1), axis=1, keepdims=True)
    carry = carry_ref[...]
    ttab_ref[...] = jnp.concatenate([carry, MOE_CHUNK * n_chunk, lstart, jnp.zeros((SUBLANES - 3, LANES), F32)], axis=0)
    carry = carry + jnp.where(i >= 1, MOE_CHUNK * n_chunk, 0.0)
    carry_ref[...] = carry
    cnt_ref[...] = jnp.broadcast_to(carry, cnt_ref.shape)

    meta = jnp.where(lane == 0, e1, jnp.where(lane == 1, e2, jnp.where(lane == 2, lo1, jnp.where(
        lane == 3, lo2, jnp.where(lane == 4, w1, jnp.where(lane == 5, w2, 0.0))))))
    meta_ref[...] = meta
    metat_ref[...] = meta.T[0:SUBLANES, :]


def _merge(o_hg, o_da, sgh, sgd, x2, wbh, wbd, wo, norm_moe, wr, br):
    t = x2.shape[0]
    tm = TM_MOE
    nt = t // tm
    cur = lambda i: jnp.minimum(i, nt - 1)
    prev = lambda i: jnp.maximum(i - 1, 0)
    row = lambda w: pl.BlockSpec((tm, w), lambda i: (cur(i), 0))
    const = lambda shape: pl.BlockSpec(shape, lambda i: (0, 0))
    return pl.pallas_call(
        _merge_kernel,
        grid=(nt + 1,),
        in_specs=[row(HG_WIDTH), row(DA_WIDTH), row(D_MODEL), row(D_MODEL), row(D_MODEL),
                  const((HG_WIDTH, D_MODEL)), const((DA_WIDTH, D_MODEL)), const((D_MODEL, D_MODEL)),
                  const((1, D_MODEL)), const((D_MODEL, LANES)), const((1, LANES))],
        out_specs=[row(D_MODEL), row(D_MODEL),
                   pl.BlockSpec((tm, LANES), lambda i: (prev(i), 0)),
                   pl.BlockSpec((SUBLANES, tm), lambda i: (0, prev(i))),
                   pl.BlockSpec((SUBLANES, LANES), lambda i: (prev(i), 0)),
                   const((SUBLANES, LANES))],
        out_shape=[jax.ShapeDtypeStruct((t, D_MODEL), F32),
                   jax.ShapeDtypeStruct((t, D_MODEL), BF16),
                   jax.ShapeDtypeStruct((t, LANES), F32),
                   jax.ShapeDtypeStruct((SUBLANES, t), F32),
                   jax.ShapeDtypeStruct((nt * SUBLANES, LANES), F32),
                   jax.ShapeDtypeStruct((SUBLANES, LANES), F32)],
        scratch_shapes=[pltpu.VMEM((1, LANES), F32), pltpu.VMEM((tm, LANES), F32)],
        compiler_params=_cparams(("arbitrary",)),
        name="merge",
    )(o_hg, o_da, sgh, sgd, x2, wbh, wbd, wo, norm_moe, wr, br)


def _for_each_chunk(tab_ref, tile, fn, group=MOE_DMA_GROUP):
    def one(c, lane):
        fn(pl.multiple_of(c * MOE_CHUNK, MOE_CHUNK), pl.multiple_of(tab_ref[tile, c], MOE_CHUNK), lane)

    def per_group(g, carry):
        for u in range(group):
            one(g * group + u, u % 2)
        return carry

    def per_chunk(c, carry):
        one(c, 0)
        return carry

    n = tab_ref[tile, TAB_COUNT]
    n_grouped = lax.div(n, group) * group
    lax.fori_loop(0, lax.div(n, group), per_group, 0)
    lax.fori_loop(n_grouped, n, per_chunk, 0)


def _dispatch_kernel(tab_ref, tail_ref, metat_ref, xn_ref, xg_hbm, xp_ref, zero_ref, sem, sem_z):
    i = pl.program_id(0)
    last = pl.num_programs(0) - 1
    lo1 = metat_ref[2:3, :]
    lo2 = metat_ref[3:4, :]
    xn = xn_ref[...]
    for r0 in range(0, MOE_LOCAL, MOE_KBLOCK):
        r = (lax.broadcasted_iota(jnp.int32, (MOE_KBLOCK, xn.shape[0]), 0) + r0).astype(F32)
        perm = jnp.where((r == lo1) | (r == lo2), 1.0, 0.0).astype(BF16)
        xp_ref[i % 2, r0:r0 + MOE_KBLOCK, :] = jnp.dot(perm, xn, preferred_element_type=F32).astype(BF16)

    def chunk_copy(tile, lrow, grow):
        return pltpu.make_async_copy(xp_ref.at[tile % 2, pl.ds(lrow, MOE_CHUNK), :],
                                     xg_hbm.at[pl.ds(grow, MOE_CHUNK), :], sem.at[tile % 2])

    _for_each_chunk(tab_ref, i, lambda lrow, grow, lane: chunk_copy(i, lrow, grow).start(priority=lane))

    @pl.when(i > 0)
    def _():
        _for_each_chunk(tab_ref, i - 1, lambda lrow, grow, lane: chunk_copy(i - 1, lrow, grow).wait())

    def tail_copy(e, c):
        grow = pl.multiple_of(tail_ref[e] + c * MOE_CHUNK, MOE_CHUNK)
        return pltpu.make_async_copy(zero_ref.at[pl.ds(0, MOE_CHUNK), :], xg_hbm.at[pl.ds(grow, MOE_CHUNK), :], sem_z)

    def for_each_tail(fn):
        def per_expert(e, carry):
            lax.fori_loop(0, tail_ref[N_EXPERTS + e], lambda c, c2: (fn(e, c), c2)[1], 0)
            return carry
        lax.fori_loop(0, N_EXPERTS, per_expert, 0)

    def block_copy(b):
        grow = pl.multiple_of(b * TM_EXPERT, TM_EXPERT)
        return pltpu.make_async_copy(zero_ref, xg_hbm.at[pl.ds(grow, TM_EXPERT), :], sem_z)

    def for_each_free_block(fn):
        lax.fori_loop(tail_ref[2 * N_EXPERTS], xg_hbm.shape[0] // TM_EXPERT, lambda b, c2: (fn(b), c2)[1], 0)

    @pl.when(i == last)
    def _():
        zero_ref[...] = jnp.zeros_like(zero_ref)
        for_each_tail(lambda e, c: tail_copy(e, c).start())
        for_each_free_block(lambda b: block_copy(b).start())
        for_each_tail(lambda e, c: tail_copy(e, c).wait())
        for_each_free_block(lambda b: block_copy(b).wait())
        _for_each_chunk(tab_ref, i, lambda lrow, grow, lane: chunk_copy(i, lrow, grow).wait())


def _dispatch(tab, tail, metat, xn, n_rows):
    t = xn.shape[0]
    tm = TM_MOE
    grid_spec = pltpu.PrefetchScalarGridSpec(
        num_scalar_prefetch=2,
        grid=(t // tm,),
        in_specs=[pl.BlockSpec((SUBLANES, tm), lambda i, tab, tail: (0, i)),
                  pl.BlockSpec((tm, D_MODEL), lambda i, tab, tail: (i, 0))],
        out_specs=pl.BlockSpec(memory_space=pl.ANY),
        scratch_shapes=[pltpu.VMEM((2, MOE_LOCAL, D_MODEL), BF16), pltpu.VMEM((TM_EXPERT, D_MODEL), BF16),
                        pltpu.SemaphoreType.DMA((2,)), pltpu.SemaphoreType.DMA],
    )
    return pl.pallas_call(
        _dispatch_kernel,
        grid_spec=grid_spec,
        out_shape=jax.ShapeDtypeStruct((n_rows, D_MODEL), BF16),
        compiler_params=_cparams(("arbitrary",)),
        name="dispatch",
    )(tab, tail, metat, xn)


def _expert_kernel(be_ref, nact_ref, xg_ref, w1_ref, w3_ref, w2_ref, y_ref, w1b_ref, w3b_ref, w2b_ref):
    i = pl.program_id(0)
    active = i < nact_ref[0]

    @pl.when(active & ((i == 0) | (be_ref[i] != be_ref[jnp.maximum(i - 1, 0)])))
    def _():
        w1b_ref[...] = w1_ref[0].astype(BF16)
        w3b_ref[...] = w3_ref[0].astype(BF16)
        w2b_ref[...] = w2_ref[0].astype(BF16)

    @pl.when(active)
    def _():
        xb = xg_ref[...]
        a = jnp.dot(xb, w1b_ref[...], preferred_element_type=F32)
        b = jnp.dot(xb, w3b_ref[...], preferred_element_type=F32)
        hid = (a * _sigmoid(a) * b).astype(BF16)
        y_ref[...] = jnp.dot(hid, w2b_ref[...], preferred_element_type=F32).astype(BF16)

    @pl.when(jnp.logical_not(active))
    def _():
        y_ref[...] = jnp.zeros_like(y_ref)


def _experts(block_expert, n_active, xg, w1, w3, w2):
    tm = TM_EXPERT
    nb = xg.shape[0] // tm
    blk = lambda i, be, na: jnp.minimum(i, na[0] - 1)
    grid_spec = pltpu.PrefetchScalarGridSpec(
        num_scalar_prefetch=2,
        grid=(nb,),
        in_specs=[pl.BlockSpec((tm, D_MODEL), lambda i, be, na: (blk(i, be, na), 0)),
                  pl.BlockSpec((1, D_MODEL, D_FF), lambda i, be, na: (be[blk(i, be, na)], 0, 0)),
                  pl.BlockSpec((1, D_MODEL, D_FF), lambda i, be, na: (be[blk(i, be, na)], 0, 0)),
                  pl.BlockSpec((1, D_FF, D_MODEL), lambda i, be, na: (be[blk(i, be, na)], 0, 0))],
        out_specs=pl.BlockSpec((tm, D_MODEL), lambda i, be, na: (i, 0)),
        scratch_shapes=[pltpu.VMEM((D_MODEL, D_FF), BF16), pltpu.VMEM((D_MODEL, D_FF), BF16),
                        pltpu.VMEM((D_FF, D_MODEL), BF16)],
    )
    return pl.pallas_call(
        _expert_kernel,
        grid_spec=grid_spec,
        out_shape=jax.ShapeDtypeStruct(xg.shape, BF16),
        compiler_params=_cparams(("arbitrary",)),
        name="experts",
    )(block_expert, n_active, xg, w1, w3, w2)


def _combine_kernel(tab_ref, meta_ref, xmid_ref, y_hbm, o_ref, yl_ref, sem):
    tm = xmid_ref.shape[0]
    i = pl.program_id(0)

    def chunk_copy(tile, lrow, grow):
        return pltpu.make_async_copy(y_hbm.at[pl.ds(grow, MOE_CHUNK), :],
                                     yl_ref.at[tile % 2, pl.ds(lrow, MOE_CHUNK), :], sem.at[tile % 2])

    @pl.when(i == 0)
    def _():
        yl_ref[...] = jnp.zeros_like(yl_ref)
        _for_each_chunk(tab_ref, i, lambda lrow, grow, lane: chunk_copy(i, lrow, grow).start(priority=lane))

    @pl.when(i + 1 < pl.num_programs(0))
    def _():
        _for_each_chunk(tab_ref, i + 1,
                        lambda lrow, grow, lane: chunk_copy(i + 1, lrow, grow).start(priority=lane))

    meta = meta_ref[...]
    _for_each_chunk(tab_ref, i, lambda lrow, grow, lane: chunk_copy(i, lrow, grow).wait())
    acc = xmid_ref[...]
    for k0 in range(0, MOE_LOCAL, MOE_KBLOCK):
        lane = (lax.broadcasted_iota(jnp.int32, (tm, MOE_KBLOCK), 1) + k0).astype(F32)
        wsel = (jnp.where(lane == meta[:, 2:3], meta[:, 4:5], 0.0)
                + jnp.where(lane == meta[:, 3:4], meta[:, 5:6], 0.0)).astype(BF16)
        acc = acc + jnp.dot(wsel, yl_ref[i % 2, k0:k0 + MOE_KBLOCK, :], preferred_element_type=F32)
    o_ref[...] = acc


def _combine(tab, meta, xmid, y):
    t = xmid.shape[0]
    tm = TM_MOE
    nt = t // tm
    grid_spec = pltpu.PrefetchScalarGridSpec(
        num_scalar_prefetch=1,
        grid=(nt,),
        in_specs=[pl.BlockSpec((tm, LANES), lambda i, tab: (i, 0)),
                  pl.BlockSpec((tm, D_MODEL), lambda i, tab: (i, 0)),
                  pl.BlockSpec(memory_space=pl.ANY)],
        out_specs=pl.BlockSpec((tm, D_MODEL), lambda i, tab: (i, 0)),
        scratch_shapes=[pltpu.VMEM((2, MOE_LOCAL, D_MODEL), BF16), pltpu.SemaphoreType.DMA((2,))],
    )
    return pl.pallas_call(
        _combine_kernel,
        grid_spec=grid_spec,
        out_shape=jax.ShapeDtypeStruct((t, D_MODEL), F32),
        compiler_params=_cparams(("arbitrary",)),
        name="combine",
    )(tab, meta, xmid, y)


def kernel(x, norm_mix, w_in, hg_lb, hg_out_norm, da_q_norm, da_k_norm, da_lambda, da_out_norm, w_branch_hg,
           w_branch_da, w_out, norm_moe, w_router_group, b_router_group, w_router_expert, b_router_expert,
           w1, w3, w2):
    batch, seq, d = x.shape
    assert d == D_MODEL and norm_mix.shape[0] == 1 and w_in.shape[2] == IN_COLS
    assert seq % TM_PROJ == 0 and seq % TQ_ATTN == 0 and (batch * seq) % TM_MOE == 0
    t = batch * seq
    x2 = x.reshape(t, d)

    half = DA_HEAD // 2
    inv = ROPE_THETA ** (-jnp.arange(half, dtype=F32) / half)
    ang = jnp.arange(seq, dtype=F32)[:, None] * inv[None, :]
    cos_t = jnp.tile(jnp.cos(ang), (1, 2 * LANES // DA_HEAD))
    sin_t = jnp.tile(jnp.concatenate([-jnp.sin(ang), jnp.sin(ang)], axis=1), (1, LANES // DA_HEAD))

    reps = DA_WIDTH // DA_HEAD
    gq = jnp.tile(da_q_norm[0].astype(F32) * (DA_HEAD ** -0.5 * math.log2(math.e)), reps)[None, :]
    gk = jnp.tile(da_k_norm[0].astype(F32), reps)[None, :]

    o_hg, qn, kn, dv, sgh, sgd = _in_proj(
        x2, norm_mix.astype(F32), w_in[0].astype(BF16), hg_lb.astype(F32), gq, gk, cos_t, sin_t,
        hg_out_norm.astype(F32), seq)
    o_da = _attn(da_lambda[0].astype(F32), qn, kn, dv, da_out_norm.astype(F32), batch, seq)

    wr = jnp.zeros((D_MODEL, LANES), F32)
    wr = wr.at[:, :N_GROUPS].set(w_router_group[0]).at[:, N_GROUPS:N_GROUPS + N_EXPERTS].set(w_router_expert[0])
    br = jnp.zeros((1, LANES), F32)
    br = br.at[0, :N_GROUPS].set(b_router_group[0]).at[0, N_GROUPS:N_GROUPS + N_EXPERTS].set(b_router_expert[0])

    xmid, xn, meta, metat, ttab, cnt = _merge(
        o_hg, o_da, sgh, sgd, x2, w_branch_hg[0].astype(BF16), w_branch_da[0].astype(BF16), w_out[0].astype(BF16),
        norm_moe.astype(F32), wr, br)

    tmx = TM_EXPERT
    nt = t // TM_MOE
    ttab = ttab.reshape(nt, SUBLANES, LANES)[:, :, :N_EXPERTS].astype(jnp.int32)
    counts = cnt[0, :N_EXPERTS].astype(jnp.int32)
    padded = (counts + tmx - 1) // tmx * tmx
    pad_end = jnp.cumsum(padded)
    pad_start = pad_end - padded
    gstart, seg_len, lstart = pad_start[None, :] + ttab[:, 0], ttab[:, 1], ttab[:, 2]
    lrow = (jnp.arange(TAB_COUNT, dtype=jnp.int32) * MOE_CHUNK)[None, :, None]
    in_seg = (lrow >= lstart[:, None, :]) & (lrow < (lstart + seg_len)[:, None, :])
    chunk_grow = jnp.sum(jnp.where(in_seg, gstart[:, None, :] + lrow - lstart[:, None, :], 0), axis=2)
    tab = jnp.concatenate([chunk_grow, jnp.sum(seg_len, axis=1, keepdims=True) // MOE_CHUNK,
                           jnp.zeros((nt, LANES - TAB_COUNT - 1), jnp.int32)], axis=1)
    tail = jnp.concatenate([pad_start + counts, (padded - counts) // MOE_CHUNK, pad_end[-1:] // tmx])
    n_rows = t * TOP_K + nt * N_EXPERTS * MOE_CHUNK + N_EXPERTS * tmx
    nb = n_rows // tmx
    block_start = jnp.arange(nb, dtype=jnp.int32) * tmx
    block_expert = jnp.minimum(jnp.sum(pad_end[None, :] <= block_start[:, None], axis=1),
                               N_EXPERTS - 1).astype(jnp.int32)
    n_active = (pad_end[-1:] // tmx).astype(jnp.int32)

    xg = _dispatch(tab, tail, metat, xn, n_rows)
    y = _experts(block_expert, n_active, xg, w1[0], w3[0], w2[0])
    out = _combine(tab, meta, xmid, y)
    return out.reshape(batch, seq, d)
```
